```python
import math
import functools
import jax
import jax.numpy as jnp
from jax import lax
import numpy as np

D_MODEL = 1024
BATCH = 8
SEQ = 2048
DEPTH = 2
DEC_BATCH = 32
DEC_SEQ = 4
PAST_LEN = 8192
PAGE_SIZE = 128

F32 = jnp.float32
D_MIX = D_MODEL
H_A = 4
DK_A = D_MIX // 16
DV_A = D_MIX // 16
D_A = H_A * DV_A
CHUNK_A = 64
H_B = 4
DK_B = D_MIX // 16
DV_B = 2 * DK_B
D_B = H_B * DV_B
ROT_DIM = DK_B // 4
ROPE_THETA = 500000.0
Q_BLOCK = 128
G_C = 4
D_C = D_MIX // 4
CG_C = D_C // G_C
CHUNK_C = 128
D_FF = 2816
CONV_W = 3
N_IN = 2 * H_A * DK_A + 2 * D_A + 4 * H_B * DK_B + D_B + 2 * D_C

kernel_name = 'hymba_hgrn2_diffattn_gmlp_convffn_step'


def _rms_norm(x, g, eps=1e-6):
    xf = x.astype(F32)
    y = xf * lax.rsqrt(jnp.mean(xf * xf, axis=-1, keepdims=True) + eps)
    return (y * g.astype(F32)).astype(x.dtype)


def _rope_partial(x, pos):
    half = ROT_DIM // 2
    inv = ROPE_THETA ** (-jnp.arange(half, dtype=F32) * (2.0 / ROT_DIM))
    ang = pos.astype(F32)[:, None] * inv[None, :]
    cos = jnp.cos(ang)[None, :, None, :]
    sin = jnp.sin(ang)[None, :, None, :]
    x1 = x[..., :half].astype(F32)
    x2 = x[..., half:ROT_DIM].astype(F32)
    rot = jnp.concatenate([x1 * cos - x2 * sin, x2 * cos + x1 * sin], axis=-1).astype(x.dtype)
    return jnp.concatenate([rot, x[..., ROT_DIM:]], axis=-1)


def _gated_linear_recurrence(q, log_f, k, v, s0):
    b_, t_, h_, _ = q.shape
    dv = v.shape[-1]
    L = min(CHUNK_A, t_)
    n = -(-t_ // L)
    pad = n * L - t_

    def chunks(a):
        a = jnp.pad(a, ((0, 0), (0, pad), (0, 0), (0, 0)))
        return a.reshape(b_, n, L, h_, a.shape[-1]).transpose(1, 0, 2, 3, 4)

    causal = jnp.tril(jnp.ones((L, L), dtype=bool))[None, :, :, None, None]

    def step(S, xs):
        qc, lfc, kc, vc = xs
        cum = jnp.cumsum(lfc, axis=1)
        rel = jnp.where(causal, cum[:, :, None] - cum[:, None, :], -jnp.inf)
        decay = jnp.exp(rel)
        scores = jnp.einsum('bthc,btshc,bshc->bhts', qc, decay, kc)
        o = (jnp.einsum('bhts,bshv->bthv', scores, vc)
             + jnp.einsum('bthc,bhcv->bthv', qc * jnp.exp(cum), S))
        last = cum[:, -1]
        S_new = (jnp.exp(last)[..., None] * S
                 + jnp.einsum('bshc,bshv->bhcv', kc * jnp.exp(last[:, None] - cum), vc))
        return S_new, o

    S, o = lax.scan(step, s0, (chunks(q), chunks(log_f), chunks(k), chunks(v)))
    o = o.transpose(1, 0, 2, 3, 4).reshape(b_, n * L, h_, dv)[:, :t_]
    return o, S


def _hgrn2(qa, fa, ia, ga, lb, norm_g, s0):
    b_, t_, _ = qa.shape
    shp = (b_, t_, H_A, DK_A)
    z = fa.reshape(shp).astype(F32)
    lbh = lb.reshape(H_A, DK_A)
    log_f = jnp.logaddexp(jnp.log(lbh), jnp.log1p(-lbh) + jax.nn.log_sigmoid(z))
    k = (1.0 - lbh) * jax.nn.sigmoid(-z)
    o, S = _gated_linear_recurrence(qa.reshape(shp).astype(F32), log_f, k,
                                    ia.reshape(b_, t_, H_A, DV_A).astype(F32), s0.astype(F32))
    o = _rms_norm(o, norm_g) * jax.nn.silu(ga.reshape(b_, t_, H_A, DV_A).astype(F32))
    return o.reshape(b_, t_, D_A).astype(qa.dtype), S.astype(qa.dtype)


def _diff_weights(s, lam):
    p = jax.nn.softmax(s, axis=-1)
    p = p.reshape(s.shape[0], H_B, 2, s.shape[2], s.shape[3])
    return p[:, :, 0] - lam * p[:, :, 1]


def _diff_attn_prompt(q, k, v, lam):
    b_, t_ = q.shape[:2]
    nb = t_ // Q_BLOCK
    qb = q.reshape(b_, nb, Q_BLOCK, 2 * H_B, DK_B).transpose(1, 0, 2, 3, 4)
    kpos = jnp.arange(t_)
    scale = DK_B ** -0.5

    def block(args):
        qi, bi = args
        s = jnp.einsum('bqhd,bkhd->bhqk', qi, k).astype(F32) * scale
        qpos = bi * Q_BLOCK + jnp.arange(Q_BLOCK)
        s = jnp.where(kpos[None, :] <= qpos[:, None], s, -jnp.inf)
        a = _diff_weights(s, lam).astype(v.dtype)
        return jnp.einsum('bhqk,bkhe->bqhe', a, v)

    o = lax.map(block, (qb, jnp.arange(nb)))
    return o.transpose(1, 0, 2, 3, 4).reshape(b_, t_, H_B, DV_B)


def _diff_attn_sample(q, k, v, lam, k_past, v_past):
    t_ = q.shape[1]
    p_ = k_past.shape[1]
    scale = DK_B ** -0.5
    s_past = jnp.einsum('bqhd,bkhd->bhqk', q, k_past).astype(F32) * scale
    s_new = jnp.einsum('bqhd,bkhd->bhqk', q, k).astype(F32) * scale
    s_new = jnp.where(jnp.tril(jnp.ones((t_, t_), dtype=bool)), s_new, -jnp.inf)
    a = _diff_weights(jnp.concatenate([s_past, s_new], axis=-1), lam).astype(v.dtype)
    return (jnp.einsum('bhqk,bkhe->bqhe', a[..., :p_], v_past)
            + jnp.einsum('bhqk,bkhe->bqhe', a[..., p_:], v))


def _chunk_gmlp(u, v, ln_g, ln_b, w_s, b_s):
    b_, t_, _ = u.shape
    vf = v.astype(F32)
    mu = jnp.mean(vf, axis=-1, keepdims=True)
    var = jnp.mean(jnp.square(vf - mu), axis=-1, keepdims=True)
    vn = (vf - mu) * lax.rsqrt(var + 1e-5) * ln_g.astype(F32) + ln_b.astype(F32)
    n = -(-t_ // CHUNK_C)
    pad = n * CHUNK_C - t_
    vc = jnp.pad(vn, ((0, 0), (0, pad), (0, 0))).reshape(b_, n, CHUNK_C, G_C, CG_C)
    w_causal = w_s.astype(F32) * jnp.tril(jnp.ones((CHUNK_C, CHUNK_C), F32))
    mixed = (jnp.einsum('gts,bnsgc->bntgc', w_causal, vc)
             + b_s.astype(F32).T[None, None, :, :, None])
    mixed = mixed.reshape(b_, n * CHUNK_C, D_C)[:, :t_]
    return (u.astype(F32) * mixed).astype(u.dtype), vn.astype(v.dtype)


def _conv_ffn(x, w_up, conv_w, conv_b, w_down, prev):
    t_ = x.shape[1]
    up = x @ w_up
    ext = jnp.concatenate([prev.astype(up.dtype), up], axis=1)
    y = conv_b + sum(conv_w[j] * ext[:, j:j + t_] for j in range(CONV_W))
    gate, val = jnp.split(y, 2, axis=-1)
    h = jax.nn.gelu(gate, approximate=True) * val
    return h @ w_down, ext[:, t_:]


def _layer(x, pos, s0, conv_prev, attend, lb, lam_init, lw):
    b_, t_, _ = x.shape
    widths = [H_A * DK_A, H_A * DK_A, D_A, D_A,
              2 * H_B * DK_B, 2 * H_B * DK_B, D_B, D_C, D_C]
    idx = [int(c) for c in np.cumsum(widths)[:-1]]
    h = _rms_norm(x, lw['norm_mix_pre'])
    proj = h @ lw['w_in']
    qa, fa, ia, ga, qb, kb, vb, uc, vc = jnp.split(proj, idx, axis=-1)
    oa, s_new = _hgrn2(qa, fa, ia, ga, lb, lw['hgrn_norm'], s0)
    qb = _rope_partial(qb.reshape(b_, t_, 2 * H_B, DK_B), pos)
    kb = _rope_partial(kb.reshape(b_, t_, 2 * H_B, DK_B), pos)
    vb = vb.reshape(b_, t_, H_B, DV_B)
    lam = (jnp.exp(jnp.sum(lw['lam_q1'].astype(F32) * lw['lam_k1'].astype(F32)))
           - jnp.exp(jnp.sum(lw['lam_q2'].astype(F32) * lw['lam_k2'].astype(F32))) + lam_init)
    ob = attend(qb, kb, vb, lam)
    ob = _rms_norm(ob, lw['diff_norm']) * (1.0 - lam_init)
    oc, vn = _chunk_gmlp(uc, vc, lw['gmlp_ln_g'], lw['gmlp_ln_b'], lw['gmlp_ws'], lw['gmlp_bs'])
    mix = jnp.concatenate([oa, ob.reshape(b_, t_, D_B).astype(x.dtype), oc], axis=-1) @ lw['w_out']
    x = x + _rms_norm(mix, lw['norm_mix_post'])
    f, conv_tail = _conv_ffn(_rms_norm(x, lw['norm_ffn_pre']), lw['w_up'], lw['conv_w'],
                             lw['conv_b'], lw['w_down'], conv_prev)
    x = x + _rms_norm(f, lw['norm_ffn_post'])
    return x, kb, vb, s_new, vn, conv_tail


def setup_inputs(seed: int = 0) -> dict:
    key = jax.random.key(seed)
    ks = jax.random.split(key, 32)
    n_pages = PAST_LEN // PAGE_SIZE
    n_used = DEC_BATCH * n_pages
    n_pool = (5 * n_used) // 4

    def nrm(k, shape, scale=1.0):
        return jax.random.normal(k, shape, F32) * scale

    def gain(k, shape):
        return 1.0 + nrm(k, shape, 0.02)

    page_table = jax.random.permutation(ks[4], n_pool)[:n_used].reshape(DEC_BATCH, n_pages).astype(jnp.int32)
    return {
        'x_prompt': nrm(ks[0], (BATCH, SEQ, D_MODEL)),
        'x_sample': nrm(ks[1], (DEC_BATCH, DEC_SEQ, D_MODEL)),
        'cache_k': nrm(ks[2], (DEPTH, n_pool, PAGE_SIZE, 2 * H_B, DK_B)),
        'cache_v': nrm(ks[3], (DEPTH, n_pool, PAGE_SIZE, H_B, DV_B)),
        'page_table': page_table,
        'state_hgrn': nrm(ks[5], (DEPTH, DEC_BATCH, H_A, DK_A, DV_A), 0.5),
        'state_conv': nrm(ks[6], (DEPTH, DEC_BATCH, CONV_W - 1, 2 * D_FF)),
        'norm_mix_pre': gain(ks[7], (DEPTH, D_MODEL)),
        'norm_mix_post': gain(ks[8], (DEPTH, D_MODEL)),
        'norm_ffn_pre': gain(ks[9], (DEPTH, D_MODEL)),
        'norm_ffn_post': gain(ks[10], (DEPTH, D_MODEL)),
        'w_in': nrm(ks[11], (DEPTH, D_MODEL, N_IN), D_MODEL ** -0.5),
        'hgrn_lb': nrm(ks[12], (DEPTH, H_A * DK_A), 0.1),
        'hgrn_norm': gain(ks[13], (DEPTH, DV_A)),
        'lam_q1': nrm(ks[14], (DEPTH, DK_B), 0.1),
        'lam_k1': nrm(ks[15], (DEPTH, DK_B), 0.1),
        'lam_q2': nrm(ks[16], (DEPTH, DK_B), 0.1),
        'lam_k2': nrm(ks[17], (DEPTH, DK_B), 0.1),
        'diff_norm': gain(ks[18], (DEPTH, DV_B)),
        'gmlp_ln_g': gain(ks[19], (DEPTH, D_C)),
        'gmlp_ln_b': nrm(ks[20], (DEPTH, D_C), 0.02),
        'gmlp_ws': nrm(ks[21], (DEPTH, G_C, CHUNK_C, CHUNK_C), CHUNK_C ** -0.5),
        'gmlp_bs': 1.0 + nrm(ks[22], (DEPTH, G_C, CHUNK_C), 0.02),
        'w_out': nrm(ks[23], (DEPTH, D_MIX, D_MODEL), D_MIX ** -0.5),
        'w_up': nrm(ks[24], (DEPTH, D_MODEL, 2 * D_FF), D_MODEL ** -0.5),
        'conv_w': nrm(ks[25], (DEPTH, CONV_W, 2 * D_FF), CONV_W ** -0.5),
        'conv_b': nrm(ks[26], (DEPTH, 2 * D_FF), 0.02),
        'w_down': nrm(ks[27], (DEPTH, D_FF, D_MODEL), D_FF ** -0.5),
    }


def reference(x_prompt, x_sample, cache_k, cache_v, page_table, state_hgrn, state_conv,
              norm_mix_pre, norm_mix_post, norm_ffn_pre, norm_ffn_post, w_in, hgrn_lb,
              hgrn_norm, lam_q1, lam_k1, lam_q2, lam_k2, diff_norm, gmlp_ln_g, gmlp_ln_b,
              gmlp_ws, gmlp_bs, w_out, w_up, conv_w, conv_b, w_down):
    bp, tp, _ = x_prompt.shape
    bs, ts, _ = x_sample.shape
    past = page_table.shape[1] * cache_k.shape[2]
    pos_p = jnp.arange(tp)
    pos_s = past + jnp.arange(ts)
    lb_soft = jax.nn.softmax(hgrn_lb.astype(F32), axis=0)
    lb_all = jnp.clip(jnp.cumsum(lb_soft, axis=0) - lb_soft[0], 0.0, 1.0 - 1e-6)
    s0_p = jnp.zeros((bp, H_A, DK_A, DV_A), F32)
    conv0_p = jnp.zeros((bp, CONV_W - 1, 2 * D_FF), x_prompt.dtype)
    hp, hs = x_prompt, x_sample
    k_p, v_p, k_s, v_s = [], [], [], []
    hg_p, hg_s, gv_s, cv_p, cv_s = [], [], [], [], []
    for l in range(DEPTH):
        lw = {
            'norm_mix_pre': norm_mix_pre[l], 'norm_mix_post': norm_mix_post[l],
            'norm_ffn_pre': norm_ffn_pre[l], 'norm_ffn_post': norm_ffn_post[l],
            'w_in': w_in[l], 'hgrn_norm': hgrn_norm[l],
            'lam_q1': lam_q1[l], 'lam_k1': lam_k1[l], 'lam_q2': lam_q2[l], 'lam_k2': lam_k2[l],
            'diff_norm': diff_norm[l], 'gmlp_ln_g': gmlp_ln_g[l], 'gmlp_ln_b': gmlp_ln_b[l],
            'gmlp_ws': gmlp_ws[l], 'gmlp_bs': gmlp_bs[l], 'w_out': w_out[l],
            'w_up': w_up[l], 'conv_w': conv_w[l], 'conv_b': conv_b[l], 'w_down': w_down[l],
        }
        lam_init = 0.8 - 0.6 * math.exp(-0.3 * l)
        hp, kl, vl, sl, _, cl = _layer(hp, pos_p, s0_p, conv0_p, _diff_attn_prompt,
                                       lb_all[l], lam_init, lw)
        k_p.append(kl)
        v_p.append(vl)
        hg_p.append(sl)
        cv_p.append(cl)
        k_past = jnp.take(cache_k[l], page_table, axis=0).reshape(bs, past, 2 * H_B, DK_B)
        v_past = jnp.take(cache_v[l], page_table, axis=0).reshape(bs, past, H_B, DV_B)
        attend_s = functools.partial(_diff_attn_sample, k_past=k_past, v_past=v_past)
        hs, kl, vl, sl, vnl, cl = _layer(hs, pos_s, state_hgrn[l], state_conv[l], attend_s,
                                         lb_all[l], lam_init, lw)
        k_s.append(kl)
        v_s.append(vl)
        hg_s.append(sl)
        gv_s.append(vnl)
        cv_s.append(cl)
    return (hp, hs, jnp.stack(k_p), jnp.stack(v_p), jnp.stack(k_s), jnp.stack(v_s),
            jnp.stack(hg_p), jnp.stack(hg_s), jnp.stack(gv_s), jnp.stack(cv_p), jnp.stack(cv_s))
```

```python
import functools
import math

import jax
import jax.numpy as jnp
from jax import lax
from jax.experimental import pallas as pl
from jax.experimental.pallas import tpu as pltpu

F32 = jnp.float32
BF16 = jnp.bfloat16

H_A = 4
DK_A = 64
D_A = H_A * DK_A
H_B = 4
DK_B = 64
DV_B = 128
D_QK = 2 * H_B * DK_B
D_B = H_B * DV_B
ROT_DIM = DK_B // 4
ROPE_THETA = 500000.0
G_C = 4
D_C = 256
CG_C = D_C // G_C
CHUNK_C = 128
CHUNK_A = 64
CONV_W = 3
ROW_GROUP = 8
LANES = 128
VMEM_LIMIT = 56 * 1024 * 1024
NEG_INF = float("-inf")


def _cparams(sem):
    return pltpu.CompilerParams(dimension_semantics=sem, vmem_limit_bytes=VMEM_LIMIT)


def _rms(x, g, eps=1e-6):
    return x * lax.rsqrt(jnp.mean(x * x, axis=-1, keepdims=True) + eps) * g


def _proj_kernel(x_ref, g_ref, w_ref, cos_ref, sa_ref, sb_ref,
                 hg_ref, q_ref, k_ref, v_ref, uv_ref, *, q_scale):
    h = _rms(x_ref[...], g_ref[...]).astype(BF16)

    def mm(lo, hi):
        return jnp.dot(h, w_ref[:, lo:hi], preferred_element_type=F32)

    n_hg = 4 * D_A
    hg_ref[...] = mm(0, n_hg)
    cos = cos_ref[...]
    sa = sa_ref[...]
    sb = sb_ref[...]
    half = ROT_DIM // 2

    def rope(xx):
        return (xx * cos + pltpu.roll(xx, LANES - half, 1) * sa
                + pltpu.roll(xx, half, 1) * sb)

    for j in range(D_QK // LANES):
        lo = n_hg + j * LANES
        q_ref[:, j * LANES:(j + 1) * LANES] = rope(mm(lo, lo + LANES)) * q_scale
        lo = n_hg + D_QK + j * LANES
        k_ref[:, j * LANES:(j + 1) * LANES] = rope(mm(lo, lo + LANES))
    lo = n_hg + 2 * D_QK
    v_ref[...] = mm(lo, lo + D_B)
    uv_ref[...] = mm(lo + D_B, lo + D_B + 2 * D_C)


def _proj(x, g, w_in, tabs, tm):
    n, d = x.shape
    n_in = w_in.shape[1]
    cos, sa, sb = tabs
    nt = cos.shape[0] // tm
    row = lambda i: (i, 0)
    fix = lambda i: (0, 0)
    tab = lambda i: (i % nt, 0)
    outs = [(4 * D_A, F32), (D_QK, F32), (D_QK, F32), (D_B, F32), (2 * D_C, F32)]
    return pl.pallas_call(
        functools.partial(_proj_kernel, q_scale=DK_B ** -0.5),
        grid=(n // tm,),
        in_specs=[pl.BlockSpec((tm, d), row), pl.BlockSpec((1, d), fix),
                  pl.BlockSpec((d, n_in), fix),
                  pl.BlockSpec((tm, LANES), tab), pl.BlockSpec((tm, LANES), tab),
                  pl.BlockSpec((tm, LANES), tab)],
        out_specs=[pl.BlockSpec((tm, w), row) for w, _ in outs],
        out_shape=[jax.ShapeDtypeStruct((n, w), dt) for w, dt in outs],
        compiler_params=_cparams(("parallel",)),
        name="proj",
    )(x, g, w_in, cos, sa, sb)


def _rope_tables(pos):
    half = ROT_DIM // 2
    inv = ROPE_THETA ** (-jnp.arange(half, dtype=F32) * (2.0 / ROT_DIM))
    ang = pos.astype(F32)[:, None] * inv[None, :]
    c, s = jnp.cos(ang), jnp.sin(ang)
    t = pos.shape[0]
    rest = DK_B - ROT_DIM
    cos64 = jnp.concatenate([c, c, jnp.ones((t, rest), F32)], axis=1)
    sa64 = jnp.concatenate([-s, jnp.zeros((t, half + rest), F32)], axis=1)
    sb64 = jnp.concatenate([jnp.zeros((t, half), F32), s, jnp.zeros((t, rest), F32)], axis=1)
    rep = LANES // DK_B
    return tuple(jnp.tile(a, (1, rep)) for a in (cos64, sa64, sb64))


def _hgrn_kernel(hg_ref, lb_ref, g_ref, st0_ref, oa_ref, st_ref,
                 st_s, cum_s, kk_s, v_s, o_s, *, lc, t_valid):
    c_idx = pl.program_id(1)

    @pl.when(c_idx == 0)
    def _():
        st_s[...] = st0_ref[0]

    q = hg_ref[0, :, 0:D_A]
    z = hg_ref[0, :, D_A:2 * D_A]
    vi = hg_ref[0, :, 2 * D_A:3 * D_A]
    ga = hg_ref[0, :, 3 * D_A:4 * D_A]

    log_sig = jnp.minimum(z, 0.0) - jnp.log1p(jnp.exp(-jnp.abs(z)))
    a = lb_ref[0:1, :]
    b = lb_ref[1:2, :] + log_sig
    log_f = jnp.maximum(a, b) + jnp.log1p(jnp.exp(-jnp.abs(a - b)))
    kk = lb_ref[2:3, :] * jax.nn.sigmoid(-z)
    rows = lax.broadcasted_iota(jnp.int32, (lc, 1), 0)
    if t_valid < lc:
        log_f = jnp.where(rows < t_valid, log_f, 0.0)
        kk = jnp.where(rows < t_valid, kk, 0.0)

    cum = log_f
    d = 1
    while d < lc:
        cum = cum + jnp.where(rows >= d, pltpu.roll(cum, d, 0), 0.0)
        d *= 2

    hr = lax.broadcasted_iota(jnp.int32, (D_A, D_A), 0) // DK_A
    hc = lax.broadcasted_iota(jnp.int32, (D_A, D_A), 1) // DK_A
    same_head = hr == hc
    ones_bd = same_head.astype(BF16)

    cum_s[...] = cum
    kk_s[...] = kk
    v_s[...] = vi

    st = st_s[...]
    q_dec = (q * jnp.exp(cum)).astype(BF16)
    o_s[...] = lax.dot_general(q_dec, st.astype(BF16), (((1,), (1,)), ((), ())),
                               preferred_element_type=F32)

    for sb in range(lc // ROW_GROUP):
        r0 = sb * ROW_GROUP
        cum_t = cum[r0:, :]
        q_t = q[r0:, :]
        rows_t = rows[r0:, :]

        def body(i, o_part, r0=r0, cum_t=cum_t, q_t=q_t, rows_t=rows_t):
            s = r0 + i
            cs = cum_s[pl.ds(s, 1), :]
            ks = kk_s[pl.ds(s, 1), :]
            vs = v_s[pl.ds(s, 1), :]
            dd = jnp.where(rows_t >= s, cum_t - cs, NEG_INF)
            term = (q_t * jnp.exp(dd) * ks).astype(BF16)
            p = jnp.dot(term, ones_bd, preferred_element_type=F32)
            return o_part + p * vs

        o_part = lax.fori_loop(0, ROW_GROUP, body, jnp.zeros((lc - r0, D_A), F32))
        o_s[r0:, :] = o_s[r0:, :] + o_part

    last = cum[lc - 1:lc, :]
    k_dec = (kk * jnp.exp(last - cum)).astype(BF16)
    upd = lax.dot_general(vi.astype(BF16), k_dec, (((0,), (0,)), ((), ())),
                          preferred_element_type=F32)
    st_new = st * jnp.exp(last) + jnp.where(same_head, upd, 0.0)
    st_s[...] = st_new

    o = o_s[...]
    sq = o * o
    hi = sq.astype(BF16)
    lo = (sq - hi.astype(F32)).astype(BF16)
    ms = (jnp.dot(hi, ones_bd, preferred_element_type=F32)
          + jnp.dot(lo, ones_bd, preferred_element_type=F32)) * (1.0 / DK_A)
    oa_ref[0] = o * lax.rsqrt(ms + 1e-6) * g_ref[...] * (ga * jax.nn.sigmoid(ga))

    @pl.when(c_idx == pl.num_programs(1) - 1)
    def _():
        st_ref[0] = st_new


def _hgrn(hg, lbp, g, st0, lc, t_valid):
    b, t, w = hg.shape
    return pl.pallas_call(
        functools.partial(_hgrn_kernel, lc=lc, t_valid=t_valid),
        grid=(b, t // lc),
        in_specs=[pl.BlockSpec((1, lc, w), lambda i, c: (i, c, 0)),
                  pl.BlockSpec((3, D_A), lambda i, c: (0, 0)),
                  pl.BlockSpec((1, D_A), lambda i, c: (0, 0)),
                  pl.BlockSpec((1, D_A, D_A), lambda i, c: (i, 0, 0))],
        out_specs=[pl.BlockSpec((1, lc, D_A), lambda i, c: (i, c, 0)),
                   pl.BlockSpec((1, D_A, D_A), lambda i, c: (i, 0, 0))],
        out_shape=[jax.ShapeDtypeStruct((b, t, D_A), F32),
                   jax.ShapeDtypeStruct((b, D_A, D_A), F32)],
        scratch_shapes=[pltpu.VMEM((D_A, D_A), F32)] + [pltpu.VMEM((lc, D_A), F32)] * 4,
        compiler_params=_cparams(("parallel", "arbitrary")),
        name="hgrn",
    )(hg, lbp, g, st0)


def _state_to_bd(s):
    b = s.shape[0]
    st = jnp.swapaxes(s.astype(F32), 2, 3)
    eye = jnp.eye(H_A, dtype=F32)
    return (st[:, :, :, None, :] * eye[None, :, None, :, None]).reshape(b, D_A, D_A)


def _state_from_bd(st):
    b = st.shape[0]
    return jnp.einsum('bhvhc->bhcv', st.reshape(b, H_A, DK_A, H_A, DK_A))


def _attn_kernel(lam_ref, q_ref, k_ref, v_ref, g_ref, o_ref, m_s, l_s, acc_s,
                 *, tq, out_scale):
    qi = pl.program_id(2)
    ki = pl.program_id(3)

    @pl.when(ki == 0)
    def _():
        m_s[...] = jnp.full(m_s.shape, NEG_INF, F32)
        l_s[...] = jnp.zeros(l_s.shape, F32)
        acc_s[...] = jnp.zeros(acc_s.shape, F32)

    @pl.when(ki <= qi)
    def _():
        q = q_ref[0]
        k = k_ref[0].astype(BF16)
        v = v_ref[0].astype(BF16)
        lane = lax.broadcasted_iota(jnp.int32, (1, LANES), 1)
        row = lax.broadcasted_iota(jnp.int32, (tq, tq), 0) + qi * tq
        col = lax.broadcasted_iota(jnp.int32, (tq, tq), 1) + ki * tq
        keep = col <= row
        for sub in range(2):
            sel = (lane < DK_B) if sub == 0 else (lane >= DK_B)
            qs = jnp.where(sel, q, 0.0).astype(BF16)
            s = lax.dot_general(qs, k, (((1,), (1,)), ((), ())), preferred_element_type=F32)
            s = jnp.where(keep, s, NEG_INF)
            m_prev = m_s[sub]
            m_new = jnp.maximum(m_prev, jnp.max(s, axis=-1, keepdims=True))
            alpha = jnp.exp(m_prev - m_new)
            p = jnp.exp(s - m_new)
            l_s[sub] = alpha * l_s[sub] + jnp.sum(p, axis=-1, keepdims=True)
            acc_s[sub] = alpha * acc_s[sub] + jnp.dot(p.astype(BF16), v,
                                                      preferred_element_type=F32)
            m_s[sub] = m_new

    @pl.when(ki == qi)
    def _():
        lam = lam_ref[...]
        o = acc_s[0] / l_s[0] - lam * (acc_s[1] / l_s[1])
        o_ref[0] = _rms(o, g_ref[...]) * out_scale


def _attn_prompt(q, k, v, lam, g, out_scale, tq):
    b, t, _ = q.shape
    nq = t // tq
    qmap = lambda i, j, a, c: (i, a, j)
    kmap = lambda i, j, a, c: (i, jnp.minimum(c, a), j)
    return pl.pallas_call(
        functools.partial(_attn_kernel, tq=tq, out_scale=out_scale),
        grid=(b, H_B, nq, nq),
        in_specs=[pl.BlockSpec((1, 1), lambda i, j, a, c: (0, 0)),
                  pl.BlockSpec((1, tq, LANES), qmap),
                  pl.BlockSpec((1, tq, LANES), kmap),
                  pl.BlockSpec((1, tq, LANES), kmap),
                  pl.BlockSpec((1, DV_B), lambda i, j, a, c: (0, 0))],
        out_specs=pl.BlockSpec((1, tq, LANES), qmap),
        out_shape=jax.ShapeDtypeStruct((b, t, D_B), F32),
        scratch_shapes=[pltpu.VMEM((2, tq, 1), F32), pltpu.VMEM((2, tq, 1), F32),
                        pltpu.VMEM((2, tq, DV_B), F32)],
        compiler_params=_cparams(("parallel", "parallel", "parallel", "arbitrary")),
        name="attn_prompt",
    )(lam, q, k, v, g)


def _decode_kernel(pt_ref, lam_ref, q_ref, kn_ref, vn_ref, kc_ref, vc_ref, g_ref, o_ref,
                   qbd_s, m_s, l_s, acc_s, *, t_valid, out_scale):
    p_idx = pl.program_id(1)
    n_sub = 2 * H_B
    nr = n_sub * ROW_GROUP
    hrow = lax.broadcasted_iota(jnp.int32, (nr, 1), 0) // ROW_GROUP
    hcol = lax.broadcasted_iota(jnp.int32, (1, D_QK), 1) // DK_B

    @pl.when(p_idx == 0)
    def _():
        qt = jnp.concatenate([q_ref[...]] * n_sub, axis=0)
        qbd_s[...] = jnp.where(hrow == hcol, qt, 0.0).astype(BF16)
        m_s[...] = jnp.full(m_s.shape, NEG_INF, F32)
        l_s[...] = jnp.zeros(l_s.shape, F32)
        acc_s[...] = jnp.zeros(acc_s.shape, F32)

    def update(kb, vb, keep):
        s = lax.dot_general(qbd_s[...], kb, (((1,), (1,)), ((), ())),
                            preferred_element_type=F32)
        if keep is not None:
            s = jnp.where(keep, s, NEG_INF)
        m_prev = m_s[...]
        m_new = jnp.maximum(m_prev, jnp.max(s, axis=-1, keepdims=True))
        alpha = jnp.exp(m_prev - m_new)
        p = jnp.exp(s - m_new)
        l_s[...] = alpha * l_s[...] + jnp.sum(p, axis=-1, keepdims=True)
        acc_s[...] = alpha * acc_s[...] + jnp.dot(p.astype(BF16), vb,
                                                  preferred_element_type=F32)
        m_s[...] = m_new

    update(kc_ref[...].astype(BF16), vc_ref[...].astype(BF16), None)

    @pl.when(p_idx == pl.num_programs(1) - 1)
    def _():
        page = kc_ref.shape[0]
        zpad = jnp.zeros((page - ROW_GROUP, D_QK), F32)
        kn = jnp.concatenate([kn_ref[...], zpad], axis=0).astype(BF16)
        vn = jnp.concatenate([vn_ref[...], zpad], axis=0).astype(BF16)
        r = lax.broadcasted_iota(jnp.int32, (nr, page), 0) % ROW_GROUP
        c = lax.broadcasted_iota(jnp.int32, (nr, page), 1)
        update(kn, vn, (c <= r) & (c < t_valid))
        lam = lam_ref[...]
        coef = jnp.where(hrow % 2 == 0, 1.0, -lam)
        vcol = lax.broadcasted_iota(jnp.int32, (1, D_B), 1) // DV_B
        contrib = jnp.where(vcol == hrow // 2, acc_s[...] / l_s[...] * coef, 0.0)
        o = contrib[0:ROW_GROUP]
        for h in range(1, n_sub):
            o = o + contrib[h * ROW_GROUP:(h + 1) * ROW_GROUP]
        for j in range(H_B):
            oj = o[:, j * DV_B:(j + 1) * DV_B]
            o_ref[:, j * DV_B:(j + 1) * DV_B] = _rms(oj, g_ref[...]) * out_scale


def _attn_decode(q, k_new, v_new, cache_k, cache_v, page_table, layer, lam, g, out_scale, t_valid):
    n = q.shape[0]
    nb, n_pages = page_table.shape
    page = cache_k.shape[2]
    row = lambda i, p, pt: (i, 0)
    fix = lambda i, p, pt: (0, 0)
    cmap = lambda i, p, pt: (layer, pt[i, p], 0, 0)
    nr = 2 * H_B * ROW_GROUP
    grid_spec = pltpu.PrefetchScalarGridSpec(
        num_scalar_prefetch=1,
        grid=(nb, n_pages),
        in_specs=[pl.BlockSpec((1, 1), fix),
                  pl.BlockSpec((ROW_GROUP, D_QK), row),
                  pl.BlockSpec((ROW_GROUP, D_QK), row),
                  pl.BlockSpec((ROW_GROUP, D_B), row),
                  pl.BlockSpec((None, None, page, D_QK), cmap),
                  pl.BlockSpec((None, None, page, D_B), cmap),
                  pl.BlockSpec((1, DV_B), fix)],
        out_specs=pl.BlockSpec((ROW_GROUP, D_B), row),
        scratch_shapes=[pltpu.VMEM((nr, D_QK), BF16), pltpu.VMEM((nr, 1), F32),
                        pltpu.VMEM((nr, 1), F32), pltpu.VMEM((nr, D_B), F32)],
    )
    return pl.pallas_call(
        functools.partial(_decode_kernel, t_valid=t_valid, out_scale=out_scale),
        grid_spec=grid_spec,
        out_shape=jax.ShapeDtypeStruct((n, D_B), F32),
        compiler_params=_cparams(("parallel", "arbitrary")),
        name="attn_decode",
    )(page_table, lam, q, k_new, v_new, cache_k, cache_v, g)


def _gmlp_kernel(uv_ref, w_ref, bt_ref, lg_ref, lb_ref, oc_ref, vn_ref):
    u = uv_ref[:, 0:D_C]
    v = uv_ref[:, D_C:2 * D_C]
    mu = jnp.mean(v, axis=-1, keepdims=True)
    vc = v - mu
    var = jnp.mean(vc * vc, axis=-1, keepdims=True)
    vn = vc * lax.rsqrt(var + 1e-5) * lg_ref[...] + lb_ref[...]
    vn_ref[...] = vn
    vnb = vn.astype(BF16)
    tr = lax.broadcasted_iota(jnp.int32, (CHUNK_C, CHUNK_C), 0)
    tc = lax.broadcasted_iota(jnp.int32, (CHUNK_C, CHUNK_C), 1)
    causal = tc <= tr
    grp = lax.broadcasted_iota(jnp.int32, (1, D_C), 1) // CG_C
    mixed = bt_ref[...]
    for gi in range(G_C):
        wg = jnp.where(causal, w_ref[gi], 0.0).astype(BF16)
        mg = jnp.dot(wg, vnb, preferred_element_type=F32)
        mixed = mixed + jnp.where(grp == gi, mg, 0.0)
    oc_ref[...] = u * mixed


def _gmlp(uv, w, bt, lg, lb):
    n = uv.shape[0]
    row = lambda i: (i, 0)
    fix = lambda i: (0, 0)
    return pl.pallas_call(
        _gmlp_kernel,
        grid=(n // CHUNK_C,),
        in_specs=[pl.BlockSpec((CHUNK_C, 2 * D_C), row),
                  pl.BlockSpec((G_C, CHUNK_C, CHUNK_C), lambda i: (0, 0, 0)),
                  pl.BlockSpec((CHUNK_C, D_C), fix),
                  pl.BlockSpec((1, D_C), fix), pl.BlockSpec((1, D_C), fix)],
        out_specs=[pl.BlockSpec((CHUNK_C, D_C), row), pl.BlockSpec((CHUNK_C, D_C), row)],
        out_shape=[jax.ShapeDtypeStruct((n, D_C), F32), jax.ShapeDtypeStruct((n, D_C), F32)],
        compiler_params=_cparams(("parallel",)),
        name="gmlp",
    )(uv, w, bt, lg, lb)


def _outproj_kernel(oa_ref, ob_ref, oc_ref, w_ref, x_ref, g_ref, o_ref):
    mix = jnp.dot(oa_ref[...].astype(BF16), w_ref[0:D_A, :], preferred_element_type=F32)
    mix += jnp.dot(ob_ref[...].astype(BF16), w_ref[D_A:D_A + D_B, :],
                   preferred_element_type=F32)
    mix += jnp.dot(oc_ref[...].astype(BF16), w_ref[D_A + D_B:, :],
                   preferred_element_type=F32)
    o_ref[...] = x_ref[...] + _rms(mix, g_ref[...])


def _outproj(oa, ob, oc, w_out, x, g, tm):
    n, d = x.shape
    row = lambda i: (i, 0)
    fix = lambda i: (0, 0)
    return pl.pallas_call(
        _outproj_kernel,
        grid=(n // tm,),
        in_specs=[pl.BlockSpec((tm, D_A), row), pl.BlockSpec((tm, D_B), row),
                  pl.BlockSpec((tm, D_C), row), pl.BlockSpec(w_out.shape, fix),
                  pl.BlockSpec((tm, d), row), pl.BlockSpec((1, d), fix)],
        out_specs=pl.BlockSpec((tm, d), row),
        out_shape=jax.ShapeDtypeStruct((n, d), F32),
        compiler_params=_cparams(("parallel",)),
        name="outproj",
    )(oa, ob, oc, w_out, x, g)


def _gelu_tanh(x):
    return 0.5 * x * (1.0 + jnp.tanh(math.sqrt(2.0 / math.pi) * (x + 0.044715 * (x * x * x))))


def _ffn_kernel(*refs, rows, has_prev):
    if has_prev:
        (x_ref, gpre_ref, wg_ref, wv_ref, cwg_ref, cwv_ref, cbg_ref, cbv_ref, wd_ref, gpost_ref,
         pg_ref, pv_ref, y_ref, tg_ref, tv_ref, hn_s, acc_s) = refs
    else:
        (x_ref, gpre_ref, wg_ref, wv_ref, cwg_ref, cwv_ref, cbg_ref, cbv_ref, wd_ref, gpost_ref,
         y_ref, tg_ref, tv_ref, hn_s, acc_s) = refs
    j = pl.program_id(1)

    @pl.when(j == 0)
    def _():
        hn_s[...] = _rms(x_ref[...], gpre_ref[...]).astype(BF16)
        acc_s[...] = jnp.zeros(acc_s.shape, F32)

    hn = hn_s[...]
    ridx = lax.broadcasted_iota(jnp.int32, (rows, 1), 0)

    def branch(w_ref, cw_ref, cb_ref, p_ref, t_ref):
        up = jnp.dot(hn, w_ref[...], preferred_element_type=F32)
        if has_prev:
            up = jnp.where(ridx % ROW_GROUP >= ROW_GROUP - (CONV_W - 1), p_ref[...], up)
            t_ref[...] = up
            sh1 = pltpu.roll(up, 1, 0)
            sh2 = pltpu.roll(up, 2, 0)
        else:
            t_ref[...] = up[rows - ROW_GROUP:, :]
            sh1 = jnp.where(ridx >= 1, pltpu.roll(up, 1, 0), 0.0)
            sh2 = jnp.where(ridx >= 2, pltpu.roll(up, 2, 0), 0.0)
        return cb_ref[...] + cw_ref[0:1, :] * sh2 + cw_ref[1:2, :] * sh1 + cw_ref[2:3, :] * up

    yg = branch(wg_ref, cwg_ref, cbg_ref, pg_ref if has_prev else None, tg_ref)
    yv = branch(wv_ref, cwv_ref, cbv_ref, pv_ref if has_prev else None, tv_ref)
    hcur = (_gelu_tanh(yg) * yv).astype(BF16)
    acc_s[...] += jnp.dot(hcur, wd_ref[...], preferred_element_type=F32)

    @pl.when(j == pl.num_programs(1) - 1)
    def _():
        y_ref[...] = x_ref[...] + _rms(acc_s[...], gpost_ref[...])


def _ffn(x, g_pre, w_up, conv_w, conv_b, w_down, g_post, prev, rows, tn):
    n, d = x.shape
    d_ff = w_down.shape[0]
    nff = d_ff // tn
    has_prev = prev is not None
    xmap = lambda i, j: (i, 0)
    fix = lambda i, j: (0, 0)
    gate = lambda i, j: (0, j)
    val = lambda i, j: (0, j + nff)
    in_specs = [pl.BlockSpec((rows, d), xmap), pl.BlockSpec((1, d), fix),
                pl.BlockSpec((d, tn), gate), pl.BlockSpec((d, tn), val),
                pl.BlockSpec((CONV_W, tn), gate), pl.BlockSpec((CONV_W, tn), val),
                pl.BlockSpec((1, tn), gate), pl.BlockSpec((1, tn), val),
                pl.BlockSpec((tn, d), lambda i, j: (j, 0)), pl.BlockSpec((1, d), fix)]
    args = [x, g_pre, w_up, w_up, conv_w, conv_w, conv_b, conv_b, w_down, g_post]
    if has_prev:
        in_specs += [pl.BlockSpec((rows, tn), lambda i, j: (i, j)),
                     pl.BlockSpec((rows, tn), lambda i, j: (i, j + nff))]
        args += [prev, prev]
        t_rows = rows
    else:
        t_rows = ROW_GROUP
    t_spec = pl.BlockSpec((t_rows, tn), lambda i, j: (i, j))
    nb = n // rows
    return pl.pallas_call(
        functools.partial(_ffn_kernel, rows=rows, has_prev=has_prev),
        grid=(nb, nff),
        in_specs=in_specs,
        out_specs=[pl.BlockSpec((rows, d), xmap), t_spec, t_spec],
        out_shape=[jax.ShapeDtypeStruct((n, d), F32),
                   jax.ShapeDtypeStruct((nb * t_rows, d_ff), F32),
                   jax.ShapeDtypeStruct((nb * t_rows, d_ff), F32)],
        scratch_shapes=[pltpu.VMEM((rows, d), BF16), pltpu.VMEM((rows, d), F32)],
        compiler_params=_cparams(("parallel", "arbitrary")),
        name="ffn",
    )(*args)


def _pick(n, prefs):
    for p in prefs:
        if n % p == 0:
            return p
    return n


def kernel(x_prompt, x_sample, cache_k, cache_v, page_table, state_hgrn, state_conv, norm_mix_pre, norm_mix_post, norm_ffn_pre, norm_ffn_post, w_in, hgrn_lb, hgrn_norm, lam_q1, lam_k1, lam_q2, lam_k2, diff_norm, gmlp_ln_g, gmlp_ln_b, gmlp_ws, gmlp_bs, w_out, w_up, conv_w, conv_b, w_down):
    bp, tp, d = x_prompt.shape
    bs, ts, _ = x_sample.shape
    depth = w_in.shape[0]
    d_ff = w_down.shape[1]
    n_pool, page = cache_k.shape[1], cache_k.shape[2]
    past = page_table.shape[1] * page
    assert ts <= ROW_GROUP - (CONV_W - 1) and ts <= CHUNK_C
    assert tp % CHUNK_C == 0 and (bs * ROW_GROUP) % CHUNK_C == 0

    lb_soft = jax.nn.softmax(hgrn_lb.astype(F32), axis=0)
    lb_all = jnp.clip(jnp.cumsum(lb_soft, axis=0) - lb_soft[0], 0.0, 1.0 - 1e-6)

    tabs_p = _rope_tables(jnp.arange(tp))
    pos_s = jnp.minimum(jnp.arange(ROW_GROUP), ts - 1) + past
    tabs_s = tuple(jnp.tile(a, (bs, 1)) for a in _rope_tables(pos_s))

    ck = cache_k.reshape(depth, n_pool, page, D_QK)
    cv = cache_v.reshape(depth, n_pool, page, D_B)

    ns = bs * ROW_GROUP
    xp = x_prompt.reshape(bp * tp, d)
    xs = jnp.pad(x_sample, ((0, 0), (0, ROW_GROUP - ts), (0, 0))).reshape(ns, d)

    tm_p = _pick(tp, (512, 256, 128))
    tq = _pick(tp, (512, 256, 128))
    tn = _pick(d_ff, (256, 128))
    lc_p = _pick(tp, (CHUNK_A,))

    eye_g = jnp.eye(CHUNK_C // ROW_GROUP, dtype=F32)

    outs = {k: [] for k in ("k_p", "v_p", "k_s", "v_s", "hg_p", "hg_s", "gv_s", "cv_p", "cv_s")}
    for l in range(depth):
        lam_init = 0.8 - 0.6 * math.exp(-0.3 * l)
        lam = (jnp.exp(jnp.sum(lam_q1[l].astype(F32) * lam_k1[l].astype(F32)))
               - jnp.exp(jnp.sum(lam_q2[l].astype(F32) * lam_k2[l].astype(F32)))
               + lam_init).reshape(1, 1)
        lb = lb_all[l]
        lbp = jnp.stack([jnp.log(lb), jnp.log1p(-lb), 1.0 - lb])
        g_hgrn = jnp.tile(hgrn_norm[l], H_A)[None, :]
        g_diff = diff_norm[l][None, :]
        w_in_b = w_in[l].astype(BF16)
        w_out_b = w_out[l].astype(BF16)
        w_up_b = w_up[l].astype(BF16)
        w_down_b = w_down[l].astype(BF16)
        ws = gmlp_ws[l].astype(F32)
        bsl = gmlp_bs[l].astype(F32)
        bt_p = jnp.repeat(bsl.T, CG_C, axis=1)
        ws8 = jnp.pad(ws[:, :ts, :ts], ((0, 0), (0, ROW_GROUP - ts), (0, ROW_GROUP - ts)))
        ws_s = jnp.einsum('ab,gts->gatbs', eye_g, ws8).reshape(G_C, CHUNK_C, CHUNK_C)
        bt_s = jnp.tile(jnp.repeat(bsl.T[:ROW_GROUP], CG_C, axis=1), (CHUNK_C // ROW_GROUP, 1))
        lg = gmlp_ln_g[l][None, :]
        lbn = gmlp_ln_b[l][None, :]
        cb = conv_b[l][None, :]

        hg, q, k, v, uv = _proj(xp, norm_mix_pre[l][None, :], w_in_b, tabs_p, tm_p)
        oa, st = _hgrn(hg.reshape(bp, tp, -1), lbp, g_hgrn,
                       jnp.zeros((bp, D_A, D_A), F32), lc_p, lc_p)
        seq = lambda a: a.reshape(bp, tp, -1)
        ob = _attn_prompt(seq(q), seq(k), seq(v), lam, g_diff, 1.0 - lam_init, tq)
        oc, _ = _gmlp(uv, ws, bt_p, lg, lbn)
        x1 = _outproj(oa.reshape(-1, D_A), ob.reshape(-1, D_B), oc, w_out_b, xp,
                      norm_mix_post[l][None, :], tm_p)
        xp, tg, tv = _ffn(x1, norm_ffn_pre[l][None, :], w_up_b, conv_w[l], cb, w_down_b,
                          norm_ffn_post[l][None, :], None, tp, tn)
        outs["k_p"].append(k.reshape(bp, tp, 2 * H_B, DK_B))
        outs["v_p"].append(v.reshape(bp, tp, H_B, DV_B))
        outs["hg_p"].append(_state_from_bd(st))
        tail = jnp.concatenate([tg, tv], axis=1).reshape(bp, ROW_GROUP, 2 * d_ff)
        outs["cv_p"].append(tail[:, ROW_GROUP - (CONV_W - 1):])

        hg, q, k, v, uv = _proj(xs, norm_mix_pre[l][None, :], w_in_b, tabs_s, ns)
        oa, st = _hgrn(hg.reshape(bs, ROW_GROUP, -1), lbp, g_hgrn,
                       _state_to_bd(state_hgrn[l]), ROW_GROUP, ts)
        ob = _attn_decode(q, k, v, ck, cv, page_table, l, lam, g_diff, 1.0 - lam_init, ts)
        oc, vn = _gmlp(uv, ws_s, bt_s, lg, lbn)
        x1 = _outproj(oa.reshape(-1, D_A), ob, oc, w_out_b, xs, norm_mix_post[l][None, :], ns)
        prev = jnp.pad(jnp.roll(state_conv[l].astype(F32), -1, axis=0),
                       ((0, 0), (ROW_GROUP - (CONV_W - 1), 0), (0, 0))).reshape(ns, 2 * d_ff)
        xs, tg, tv = _ffn(x1, norm_ffn_pre[l][None, :], w_up_b, conv_w[l], cb, w_down_b,
                          norm_ffn_post[l][None, :], prev, ns, tn)
        grp = lambda a: a.reshape(bs, ROW_GROUP, -1)[:, :ts]
        outs["k_s"].append(grp(k).reshape(bs, ts, 2 * H_B, DK_B))
        outs["v_s"].append(grp(v).reshape(bs, ts, H_B, DV_B))
        outs["hg_s"].append(_state_from_bd(st))
        outs["gv_s"].append(grp(vn))
        up_s = jnp.concatenate([tg, tv], axis=1).reshape(bs, ROW_GROUP, 2 * d_ff)
        outs["cv_s"].append(up_s[:, ts - (CONV_W - 1):ts])

    y_p = xp.reshape(bp, tp, d)
    y_s = xs.reshape(bs, ROW_GROUP, d)[:, :ts]
    st = lambda key: jnp.stack(outs[key])
    return (y_p, y_s, st("k_p"), st("v_p"), st("k_s"), st("v_s"), st("hg_p"), st("hg_s"),
            st("gv_s"), st("cv_p"), st("cv_s"))
```

```python
import functools
import math

import jax
import jax.numpy as jnp
from jax import lax
from jax.experimental import pallas as pl
from jax.experimental.pallas import tpu as pltpu

F32 = jnp.float32
BF16 = jnp.bfloat16

H_A = 4
DK_A = 64
D_A = H_A * DK_A
H_B = 4
DK_B = 64
DV_B = 128
D_QK = 2 * H_B * DK_B
D_B = H_B * DV_B
ROT_DIM = DK_B // 4
ROPE_THETA = 500000.0
G_C = 4
D_C = 256
CG_C = D_C // G_C
CHUNK_C = 128
CHUNK_A = 64
CONV_W = 3
ROW_GROUP = 8
LANES = 128
VMEM_LIMIT = 56 * 1024 * 1024
NEG_INF = float("-inf")


def _cparams(sem):
    return pltpu.CompilerParams(dimension_semantics=sem, vmem_limit_bytes=VMEM_LIMIT)


def _rms(x, g, eps=1e-6):
    return x * lax.rsqrt(jnp.mean(x * x, axis=-1, keepdims=True) + eps) * g


def _proj_kernel(x_ref, g_ref, w_ref, cos_ref, sa_ref, sb_ref,
                 hg_ref, q_ref, k_ref, v_ref, uv_ref, *, q_scale):
    h = _rms(x_ref[...], g_ref[...]).astype(BF16)

    def mm(lo, hi):
        return jnp.dot(h, w_ref[:, lo:hi], preferred_element_type=F32)

    n_hg = 4 * D_A
    hg_ref[...] = mm(0, n_hg)
    cos = cos_ref[...]
    sa = sa_ref[...]
    sb = sb_ref[...]
    half = ROT_DIM // 2

    def rope(xx):
        return (xx * cos + pltpu.roll(xx, LANES - half, 1) * sa
                + pltpu.roll(xx, half, 1) * sb)

    for j in range(D_QK // LANES):
        lo = n_hg + j * LANES
        q_ref[:, j * LANES:(j + 1) * LANES] = rope(mm(lo, lo + LANES)) * q_scale
        lo = n_hg + D_QK + j * LANES
        k_ref[:, j * LANES:(j + 1) * LANES] = rope(mm(lo, lo + LANES))
    lo = n_hg + 2 * D_QK
    v_ref[...] = mm(lo, lo + D_B)
    uv_ref[...] = mm(lo + D_B, lo + D_B + 2 * D_C)


def _proj(x, g, w_in, tabs, tm):
    n, d = x.shape
    n_in = w_in.shape[1]
    cos, sa, sb = tabs
    nt = cos.shape[0] // tm
    row = lambda i: (i, 0)
    fix = lambda i: (0, 0)
    tab = lambda i: (i % nt, 0)
    outs = [(4 * D_A, F32), (D_QK, F32), (D_QK, F32), (D_B, F32), (2 * D_C, F32)]
    return pl.pallas_call(
        functools.partial(_proj_kernel, q_scale=DK_B ** -0.5),
        grid=(n // tm,),
        in_specs=[pl.BlockSpec((tm, d), row), pl.BlockSpec((1, d), fix),
                  pl.BlockSpec((d, n_in), fix),
                  pl.BlockSpec((tm, LANES), tab), pl.BlockSpec((tm, LANES), tab),
                  pl.BlockSpec((tm, LANES), tab)],
        out_specs=[pl.BlockSpec((tm, w), row) for w, _ in outs],
        out_shape=[jax.ShapeDtypeStruct((n, w), dt) for w, dt in outs],
        compiler_params=_cparams(("parallel",)),
        name="proj",
    )(x, g, w_in, cos, sa, sb)


def _rope_tables(pos):
    half = ROT_DIM // 2
    inv = ROPE_THETA ** (-jnp.arange(half, dtype=F32) * (2.0 / ROT_DIM))
    ang = pos.astype(F32)[:, None] * inv[None, :]
    c, s = jnp.cos(ang), jnp.sin(ang)
    t = pos.shape[0]
    rest = DK_B - ROT_DIM
    cos64 = jnp.concatenate([c, c, jnp.ones((t, rest), F32)], axis=1)
    sa64 = jnp.concatenate([-s, jnp.zeros((t, half + rest), F32)], axis=1)
    sb64 = jnp.concatenate([jnp.zeros((t, half), F32), s, jnp.zeros((t, rest), F32)], axis=1)
    rep = LANES // DK_B
    return tuple(jnp.tile(a, (1, rep)) for a in (cos64, sa64, sb64))


def _hgrn_chunk(q, z, vi, ga, st, lb_ref, g_ref, cum_s, kk_s, v_s, *, lc, bs, t_valid):
    nblk = lc // bs
    log_sig = jnp.minimum(z, 0.0) - jnp.log1p(jnp.exp(-jnp.abs(z)))
    a = lb_ref[0:1, :]
    b = lb_ref[1:2, :] + log_sig
    log_f = jnp.maximum(a, b) + jnp.log1p(jnp.exp(-jnp.abs(a - b)))
    kk = lb_ref[2:3, :] * jax.nn.sigmoid(-z)
    rows = lax.broadcasted_iota(jnp.int32, (lc, 1), 0)
    if t_valid < lc:
        log_f = jnp.where(rows < t_valid, log_f, 0.0)
        kk = jnp.where(rows < t_valid, kk, 0.0)

    cum = log_f
    d = 1
    while d < lc:
        cum = cum + jnp.where(rows >= d, pltpu.roll(cum, d, 0), 0.0)
        d *= 2

    cum_s[...] = cum
    kk_s[...] = kk
    v_s[...] = vi

    hr = lax.broadcasted_iota(jnp.int32, (D_A, D_A), 0) // DK_A
    hc = lax.broadcasted_iota(jnp.int32, (D_A, D_A), 1) // DK_A
    same_head = hr == hc
    ones_bd = same_head.astype(BF16)

    q_dec = (q * jnp.exp(cum)).astype(BF16)
    o_inter = lax.dot_general(q_dec, st.astype(BF16), (((1,), (1,)), ((), ())),
                              preferred_element_type=F32)
    o_blk = [o_inter[i * bs:(i + 1) * bs] for i in range(nblk)]

    sub = ROW_GROUP
    for i in range(nblk):
        r0 = i * bs
        pieces = []
        for s in range(bs):
            lo = r0 + (s // sub) * sub
            cs = cum_s[r0 + s:r0 + s + 1, :]
            ks = kk_s[r0 + s:r0 + s + 1, :]
            dd = cum[lo:r0 + bs] - cs
            if s % sub:
                dd = jnp.where(rows[lo:r0 + bs] >= r0 + s, dd, NEG_INF)
            pieces.append(q[lo:r0 + bs] * jnp.exp(dd) * ks)
        p_all = jnp.dot(jnp.concatenate(pieces, axis=0).astype(BF16), ones_bd,
                        preferred_element_type=F32)
        off = 0
        acc = {}
        for s in range(bs):
            lo = (s // sub) * sub
            n = bs - lo
            contrib = p_all[off:off + n] * v_s[r0 + s:r0 + s + 1, :]
            acc[lo] = contrib if lo not in acc else acc[lo] + contrib
            off += n
        tot = acc[0]
        for lo, val in acc.items():
            if lo:
                tot = tot + jnp.concatenate([jnp.zeros((lo, D_A), F32), val], axis=0)
        o_blk[i] = o_blk[i] + tot

    if nblk > 1:
        hm = (lax.broadcasted_iota(jnp.int32, (H_A * bs, D_A), 0) // bs
              == lax.broadcasted_iota(jnp.int32, (H_A * bs, D_A), 1) // DK_A)
        for j in range(nblk - 1):
            r1 = (j + 1) * bs
            aj = cum[r1 - 1:r1, :]
            k_t = kk[r1 - bs:r1] * jnp.exp(aj - cum[r1 - bs:r1])
            q_t = (q[r1:] * jnp.exp(cum[r1:] - aj)).astype(BF16)
            k_bd = jnp.where(hm, jnp.concatenate([k_t] * H_A, axis=0), 0.0).astype(BF16)
            v_bd = jnp.where(hm, jnp.concatenate([vi[r1 - bs:r1]] * H_A, axis=0), 0.0).astype(BF16)
            s2 = lax.dot_general(q_t, k_bd, (((1,), (1,)), ((), ())),
                                 preferred_element_type=F32)
            o_off = jnp.dot(s2.astype(BF16), v_bd, preferred_element_type=F32)
            for i in range(j + 1, nblk):
                o_blk[i] = o_blk[i] + o_off[(i - j - 1) * bs:(i - j) * bs]

    last = cum[lc - 1:lc, :]
    k_dec = (kk * jnp.exp(last - cum)).astype(BF16)
    upd = lax.dot_general(vi.astype(BF16), k_dec, (((0,), (0,)), ((), ())),
                          preferred_element_type=F32)
    st_new = st * jnp.exp(last) + jnp.where(same_head, upd, 0.0)

    o = jnp.concatenate(o_blk, axis=0) if nblk > 1 else o_blk[0]
    sq = o * o
    hi = sq.astype(BF16)
    lo_ = (sq - hi.astype(F32)).astype(BF16)
    ms = (jnp.dot(hi, ones_bd, preferred_element_type=F32)
          + jnp.dot(lo_, ones_bd, preferred_element_type=F32)) * (1.0 / DK_A)
    oa = o * lax.rsqrt(ms + 1e-6) * g_ref[...] * (ga * jax.nn.sigmoid(ga))
    return oa, st_new


def _hgrn_kernel(hg_ref, lb_ref, g_ref, st0_ref, oa_ref, st_ref, st_s, cum_s, kk_s, v_s,
                 *, lc, bs, n_chunks, t_valid):
    t_idx = pl.program_id(1)

    @pl.when(t_idx == 0)
    def _():
        st_s[...] = st0_ref[0]

    def chunk(ci, carry):
        r = ci * lc if isinstance(ci, int) else pl.multiple_of(ci * lc, lc)
        blk = lambda k: hg_ref[0, pl.ds(r, lc), k * D_A:(k + 1) * D_A]
        oa, st_new = _hgrn_chunk(blk(0), blk(1), blk(2), blk(3), st_s[...], lb_ref, g_ref,
                                 cum_s, kk_s, v_s, lc=lc, bs=bs, t_valid=t_valid)
        oa_ref[0, pl.ds(r, lc), :] = oa
        st_s[...] = st_new
        return carry

    if n_chunks == 1:
        chunk(0, 0)
    else:
        lax.fori_loop(0, n_chunks, chunk, 0)

    @pl.when(t_idx == pl.num_programs(1) - 1)
    def _():
        st_ref[0] = st_s[...]


def _hgrn(hg, lbp, g, st0, lc, bs, tb, t_valid):
    b, t, w = hg.shape
    return pl.pallas_call(
        functools.partial(_hgrn_kernel, lc=lc, bs=bs, n_chunks=tb // lc, t_valid=t_valid),
        grid=(b, t // tb),
        in_specs=[pl.BlockSpec((1, tb, w), lambda i, c: (i, c, 0)),
                  pl.BlockSpec((3, D_A), lambda i, c: (0, 0)),
                  pl.BlockSpec((1, D_A), lambda i, c: (0, 0)),
                  pl.BlockSpec((1, D_A, D_A), lambda i, c: (i, 0, 0))],
        out_specs=[pl.BlockSpec((1, tb, D_A), lambda i, c: (i, c, 0)),
                   pl.BlockSpec((1, D_A, D_A), lambda i, c: (i, 0, 0))],
        out_shape=[jax.ShapeDtypeStruct((b, t, D_A), F32),
                   jax.ShapeDtypeStruct((b, D_A, D_A), F32)],
        scratch_shapes=[pltpu.VMEM((D_A, D_A), F32)] + [pltpu.VMEM((lc, D_A), F32)] * 3,
        compiler_params=_cparams(("parallel", "arbitrary")),
        name="hgrn",
    )(hg, lbp, g, st0)


def _state_to_bd(s):
    b = s.shape[0]
    st = jnp.swapaxes(s.astype(F32), 2, 3)
    eye = jnp.eye(H_A, dtype=F32)
    return (st[:, :, :, None, :] * eye[None, :, None, :, None]).reshape(b, D_A, D_A)


def _state_from_bd(st):
    b = st.shape[0]
    return jnp.einsum('bhvhc->bhcv', st.reshape(b, H_A, DK_A, H_A, DK_A))


def _attn_kernel(lam_ref, q_ref, k_ref, v_ref, g_ref, o_ref, m_s, l_s, acc_s,
                 *, tq, out_scale):
    qi = pl.program_id(2)
    ki = pl.program_id(3)

    @pl.when(ki == 0)
    def _():
        m_s[...] = jnp.full(m_s.shape, NEG_INF, F32)
        l_s[...] = jnp.zeros(l_s.shape, F32)
        acc_s[...] = jnp.zeros(acc_s.shape, F32)

    def step(masked):
        q = q_ref[0]
        k = k_ref[0].astype(BF16)
        v = v_ref[0].astype(BF16)
        lane = lax.broadcasted_iota(jnp.int32, (1, LANES), 1)
        if masked:
            keep = (lax.broadcasted_iota(jnp.int32, (tq, tq), 1)
                    <= lax.broadcasted_iota(jnp.int32, (tq, tq), 0))
        for sub in range(2):
            sel = (lane < DK_B) if sub == 0 else (lane >= DK_B)
            qs = jnp.where(sel, q, 0.0).astype(BF16)
            s = lax.dot_general(qs, k, (((1,), (1,)), ((), ())), preferred_element_type=F32)
            if masked:
                s = jnp.where(keep, s, NEG_INF)
            m_prev = m_s[sub]
            m_new = jnp.maximum(m_prev, jnp.max(s, axis=-1, keepdims=True))
            alpha = jnp.exp(m_prev - m_new)
            p = jnp.exp(s - m_new)
            l_s[sub] = alpha * l_s[sub] + jnp.sum(p, axis=-1, keepdims=True)
            acc_s[sub] = alpha * acc_s[sub] + jnp.dot(p.astype(BF16), v,
                                                      preferred_element_type=F32)
            m_s[sub] = m_new

    @pl.when(ki < qi)
    def _():
        step(False)

    @pl.when(ki == qi)
    def _():
        step(True)
        lam = lam_ref[...]
        o = acc_s[0] / l_s[0] - lam * (acc_s[1] / l_s[1])
        o_ref[0] = _rms(o, g_ref[...]) * out_scale


def _attn_prompt(q, k, v, lam, g, out_scale, tq):
    b, t, _ = q.shape
    nq = t // tq
    qmap = lambda i, j, a, c: (i, a, j)
    kmap = lambda i, j, a, c: (i, jnp.minimum(c, a), j)
    return pl.pallas_call(
        functools.partial(_attn_kernel, tq=tq, out_scale=out_scale),
        grid=(b, H_B, nq, nq),
        in_specs=[pl.BlockSpec((1, 1), lambda i, j, a, c: (0, 0)),
                  pl.BlockSpec((1, tq, LANES), qmap),
                  pl.BlockSpec((1, tq, LANES), kmap),
                  pl.BlockSpec((1, tq, LANES), kmap),
                  pl.BlockSpec((1, DV_B), lambda i, j, a, c: (0, 0))],
        out_specs=pl.BlockSpec((1, tq, LANES), qmap),
        out_shape=jax.ShapeDtypeStruct((b, t, D_B), F32),
        scratch_shapes=[pltpu.VMEM((2, tq, 1), F32), pltpu.VMEM((2, tq, 1), F32),
                        pltpu.VMEM((2, tq, DV_B), F32)],
        compiler_params=_cparams(("parallel", "parallel", "parallel", "arbitrary")),
        name="attn_prompt",
    )(lam, q, k, v, g)


def _decode_kernel(pt_ref, lam_ref, q_ref, kn_ref, vn_ref, *rest, n_grp, t_valid, out_scale):
    k_refs = rest[:n_grp]
    v_refs = rest[n_grp:2 * n_grp]
    g_ref, o_ref, qbd_s, m_s, l_s, acc_s = rest[2 * n_grp:]
    p_idx = pl.program_id(1)
    n_sub = 2 * H_B
    nr = n_sub * ROW_GROUP
    page = k_refs[0].shape[1]
    hrow = lax.broadcasted_iota(jnp.int32, (nr, 1), 0) // ROW_GROUP
    hcol = lax.broadcasted_iota(jnp.int32, (1, D_QK), 1) // DK_B

    @pl.when(p_idx == 0)
    def _():
        qt = jnp.concatenate([q_ref[...]] * n_sub, axis=0)
        qbd_s[...] = jnp.where(hrow == hcol, qt, 0.0).astype(BF16)
        m_s[...] = jnp.full(m_s.shape, NEG_INF, F32)
        l_s[...] = jnp.zeros(l_s.shape, F32)
        acc_s[...] = jnp.zeros(acc_s.shape, F32)

    def update(s, pv):
        m_prev = m_s[...]
        m_new = jnp.maximum(m_prev, jnp.max(s, axis=-1, keepdims=True))
        alpha = jnp.exp(m_prev - m_new)
        p = jnp.exp(s - m_new)
        l_s[...] = alpha * l_s[...] + jnp.sum(p, axis=-1, keepdims=True)
        acc_s[...] = alpha * acc_s[...] + pv(p.astype(BF16))
        m_s[...] = m_new

    qbd = qbd_s[...]
    s_pages = [jnp.dot(qbd, k_refs[i][...].astype(BF16), preferred_element_type=F32)
               for i in range(n_grp)]

    def pv_pages(p):
        tot = None
        for i in range(n_grp):
            vcat = jnp.concatenate(
                [v_refs[i][pl.ds(h, page, stride=H_B), :] for h in range(H_B)], axis=1)
            t = jnp.dot(p[:, i * page:(i + 1) * page], vcat.astype(BF16),
                        preferred_element_type=F32)
            tot = t if tot is None else tot + t
        return tot

    update(jnp.concatenate(s_pages, axis=1) if n_grp > 1 else s_pages[0], pv_pages)

    @pl.when(p_idx == pl.num_programs(1) - 1)
    def _():
        zpad = jnp.zeros((page - ROW_GROUP, D_QK), F32)
        kn = jnp.concatenate([kn_ref[...], zpad], axis=0).astype(BF16)
        vn = jnp.concatenate([vn_ref[...], zpad], axis=0).astype(BF16)
        r = lax.broadcasted_iota(jnp.int32, (nr, page), 0) % ROW_GROUP
        c = lax.broadcasted_iota(jnp.int32, (nr, page), 1)
        s_new = lax.dot_general(qbd, kn, (((1,), (1,)), ((), ())), preferred_element_type=F32)
        s_new = jnp.where((c <= r) & (c < t_valid), s_new, NEG_INF)
        update(s_new, lambda p: jnp.dot(p, vn, preferred_element_type=F32))
        lam = lam_ref[...]
        coef = jnp.where(hrow % 2 == 0, 1.0, -lam)
        vcol = lax.broadcasted_iota(jnp.int32, (1, D_B), 1) // DV_B
        contrib = jnp.where(vcol == hrow // 2, acc_s[...] / l_s[...] * coef, 0.0)
        o = contrib[0:ROW_GROUP]
        for h in range(1, n_sub):
            o = o + contrib[h * ROW_GROUP:(h + 1) * ROW_GROUP]
        for j in range(H_B):
            oj = o[:, j * DV_B:(j + 1) * DV_B]
            o_ref[:, j * DV_B:(j + 1) * DV_B] = _rms(oj, g_ref[...]) * out_scale


def _attn_decode(q, k_new, v_new, cache_kt, cache_v2, page_table, layer, lam, g, out_scale,
                 t_valid, n_grp):
    n = q.shape[0]
    nb, n_pages = page_table.shape
    page = cache_kt.shape[3]
    row = lambda i, p, pt: (i, 0)
    fix = lambda i, p, pt: (0, 0)

    def cmap(gi):
        return lambda i, p, pt: (layer, pt[i, p * n_grp + gi], 0, 0)

    nr = 2 * H_B * ROW_GROUP
    grid_spec = pltpu.PrefetchScalarGridSpec(
        num_scalar_prefetch=1,
        grid=(nb, n_pages // n_grp),
        in_specs=([pl.BlockSpec((1, 1), fix),
                   pl.BlockSpec((ROW_GROUP, D_QK), row),
                   pl.BlockSpec((ROW_GROUP, D_QK), row),
                   pl.BlockSpec((ROW_GROUP, D_B), row)]
                  + [pl.BlockSpec((None, None, D_QK, page), cmap(gi)) for gi in range(n_grp)]
                  + [pl.BlockSpec((None, None, page * H_B, DV_B), cmap(gi)) for gi in range(n_grp)]
                  + [pl.BlockSpec((1, DV_B), fix)]),
        out_specs=pl.BlockSpec((ROW_GROUP, D_B), row),
        scratch_shapes=[pltpu.VMEM((nr, D_QK), BF16), pltpu.VMEM((nr, 1), F32),
                        pltpu.VMEM((nr, 1), F32), pltpu.VMEM((nr, D_B), F32)],
    )
    return pl.pallas_call(
        functools.partial(_decode_kernel, n_grp=n_grp, t_valid=t_valid, out_scale=out_scale),
        grid_spec=grid_spec,
        out_shape=jax.ShapeDtypeStruct((n, D_B), F32),
        compiler_params=_cparams(("parallel", "arbitrary")),
        name="attn_decode",
    )(page_table, lam, q, k_new, v_new, *([cache_kt] * n_grp), *([cache_v2] * n_grp), g)


def _gmlp_kernel(uv_ref, w_ref, bt_ref, lg_ref, lb_ref, oc_ref, vn_ref):
    u = uv_ref[:, 0:D_C]
    v = uv_ref[:, D_C:2 * D_C]
    mu = jnp.mean(v, axis=-1, keepdims=True)
    vc = v - mu
    var = jnp.mean(vc * vc, axis=-1, keepdims=True)
    vn = vc * lax.rsqrt(var + 1e-5) * lg_ref[...] + lb_ref[...]
    vn_ref[...] = vn
    vnb = vn.astype(BF16)
    tr = lax.broadcasted_iota(jnp.int32, (CHUNK_C, CHUNK_C), 0)
    tc = lax.broadcasted_iota(jnp.int32, (CHUNK_C, CHUNK_C), 1)
    causal = tc <= tr
    grp = lax.broadcasted_iota(jnp.int32, (1, D_C), 1) // CG_C
    mixed = bt_ref[...]
    for gi in range(G_C):
        wg = jnp.where(causal, w_ref[gi], 0.0).astype(BF16)
        mg = jnp.dot(wg, vnb, preferred_element_type=F32)
        mixed = mixed + jnp.where(grp == gi, mg, 0.0)
    oc_ref[...] = u * mixed


def _gmlp(uv, w, bt, lg, lb):
    n = uv.shape[0]
    row = lambda i: (i, 0)
    fix = lambda i: (0, 0)
    return pl.pallas_call(
        _gmlp_kernel,
        grid=(n // CHUNK_C,),
        in_specs=[pl.BlockSpec((CHUNK_C, 2 * D_C), row),
                  pl.BlockSpec((G_C, CHUNK_C, CHUNK_C), lambda i: (0, 0, 0)),
                  pl.BlockSpec((CHUNK_C, D_C), fix),
                  pl.BlockSpec((1, D_C), fix), pl.BlockSpec((1, D_C), fix)],
        out_specs=[pl.BlockSpec((CHUNK_C, D_C), row), pl.BlockSpec((CHUNK_C, D_C), row)],
        out_shape=[jax.ShapeDtypeStruct((n, D_C), F32), jax.ShapeDtypeStruct((n, D_C), F32)],
        compiler_params=_cparams(("parallel",)),
        name="gmlp",
    )(uv, w, bt, lg, lb)


def _outproj_kernel(oa_ref, ob_ref, oc_ref, w_ref, x_ref, g_ref, o_ref):
    mix = jnp.dot(oa_ref[...].astype(BF16), w_ref[0:D_A, :], preferred_element_type=F32)
    mix += jnp.dot(ob_ref[...].astype(BF16), w_ref[D_A:D_A + D_B, :],
                   preferred_element_type=F32)
    mix += jnp.dot(oc_ref[...].astype(BF16), w_ref[D_A + D_B:, :],
                   preferred_element_type=F32)
    o_ref[...] = x_ref[...] + _rms(mix, g_ref[...])


def _outproj(oa, ob, oc, w_out, x, g, tm):
    n, d = x.shape
    row = lambda i: (i, 0)
    fix = lambda i: (0, 0)
    return pl.pallas_call(
        _outproj_kernel,
        grid=(n // tm,),
        in_specs=[pl.BlockSpec((tm, D_A), row), pl.BlockSpec((tm, D_B), row),
                  pl.BlockSpec((tm, D_C), row), pl.BlockSpec(w_out.shape, fix),
                  pl.BlockSpec((tm, d), row), pl.BlockSpec((1, d), fix)],
        out_specs=pl.BlockSpec((tm, d), row),
        out_shape=jax.ShapeDtypeStruct((n, d), F32),
        compiler_params=_cparams(("parallel",)),
        name="outproj",
    )(oa, ob, oc, w_out, x, g)


def _gelu_tanh(x):
    return 0.5 * x * (1.0 + jnp.tanh(math.sqrt(2.0 / math.pi) * (x + 0.044715 * (x * x * x))))


def _ffn_kernel(*refs, rows, has_prev):
    if has_prev:
        (x_ref, gpre_ref, wg_ref, wv_ref, cwg_ref, cwv_ref, cbg_ref, cbv_ref, wd_ref, gpost_ref,
         pg_ref, pv_ref, y_ref, tg_ref, tv_ref, hn_s, acc_s) = refs
    else:
        (x_ref, gpre_ref, wg_ref, wv_ref, cwg_ref, cwv_ref, cbg_ref, cbv_ref, wd_ref, gpost_ref,
         y_ref, tg_ref, tv_ref, hn_s, acc_s) = refs
    j = pl.program_id(1)

    @pl.when(j == 0)
    def _():
        hn_s[...] = _rms(x_ref[...], gpre_ref[...]).astype(BF16)
        acc_s[...] = jnp.zeros(acc_s.shape, F32)

    hn = hn_s[...]
    ridx = lax.broadcasted_iota(jnp.int32, (rows, 1), 0)

    def branch(w_ref, cw_ref, cb_ref, p_ref, t_ref):
        up = jnp.dot(hn, w_ref[...], preferred_element_type=F32)
        if has_prev:
            up = jnp.where(ridx % ROW_GROUP >= ROW_GROUP - (CONV_W - 1), p_ref[...], up)
            t_ref[...] = up
            sh1 = pltpu.roll(up, 1, 0)
            sh2 = pltpu.roll(up, 2, 0)
        else:
            t_ref[...] = up[rows - ROW_GROUP:, :]
            sh1 = jnp.where(ridx >= 1, pltpu.roll(up, 1, 0), 0.0)
            sh2 = jnp.where(ridx >= 2, pltpu.roll(up, 2, 0), 0.0)
        return cb_ref[...] + cw_ref[0:1, :] * sh2 + cw_ref[1:2, :] * sh1 + cw_ref[2:3, :] * up

    yg = branch(wg_ref, cwg_ref, cbg_ref, pg_ref if has_prev else None, tg_ref)
    yv = branch(wv_ref, cwv_ref, cbv_ref, pv_ref if has_prev else None, tv_ref)
    hcur = (_gelu_tanh(yg) * yv).astype(BF16)
    acc_s[...] += jnp.dot(hcur, wd_ref[...], preferred_element_type=F32)

    @pl.when(j == pl.num_programs(1) - 1)
    def _():
        y_ref[...] = x_ref[...] + _rms(acc_s[...], gpost_ref[...])


def _ffn(x, g_pre, w_up, conv_w, conv_b, w_down, g_post, prev, rows, tn):
    n, d = x.shape
    d_ff = w_down.shape[0]
    nff = d_ff // tn
    has_prev = prev is not None
    xmap = lambda i, j: (i, 0)
    fix = lambda i, j: (0, 0)
    gate = lambda i, j: (0, j)
    val = lambda i, j: (0, j + nff)
    in_specs = [pl.BlockSpec((rows, d), xmap), pl.BlockSpec((1, d), fix),
                pl.BlockSpec((d, tn), gate), pl.BlockSpec((d, tn), val),
                pl.BlockSpec((CONV_W, tn), gate), pl.BlockSpec((CONV_W, tn), val),
                pl.BlockSpec((1, tn), gate), pl.BlockSpec((1, tn), val),
                pl.BlockSpec((tn, d), lambda i, j: (j, 0)), pl.BlockSpec((1, d), fix)]
    args = [x, g_pre, w_up, w_up, conv_w, conv_w, conv_b, conv_b, w_down, g_post]
    if has_prev:
        in_specs += [pl.BlockSpec((rows, tn), lambda i, j: (i, j)),
                     pl.BlockSpec((rows, tn), lambda i, j: (i, j + nff))]
        args += [prev, prev]
        t_rows = rows
    else:
        t_rows = ROW_GROUP
    t_spec = pl.BlockSpec((t_rows, tn), lambda i, j: (i, j))
    nb = n // rows
    return pl.pallas_call(
        functools.partial(_ffn_kernel, rows=rows, has_prev=has_prev),
        grid=(nb, nff),
        in_specs=in_specs,
        out_specs=[pl.BlockSpec((rows, d), xmap), t_spec, t_spec],
        out_shape=[jax.ShapeDtypeStruct((n, d), F32),
                   jax.ShapeDtypeStruct((nb * t_rows, d_ff), F32),
                   jax.ShapeDtypeStruct((nb * t_rows, d_ff), F32)],
        scratch_shapes=[pltpu.VMEM((rows, d), BF16), pltpu.VMEM((rows, d), F32)],
        compiler_params=_cparams(("parallel", "arbitrary")),
        name="ffn",
    )(*args)


def _pick(n, prefs):
    for p in prefs:
        if n % p == 0:
            return p
    return n


def kernel(x_prompt, x_sample, cache_k, cache_v, page_table, state_hgrn, state_conv, norm_mix_pre, norm_mix_post, norm_ffn_pre, norm_ffn_post, w_in, hgrn_lb, hgrn_norm, lam_q1, lam_k1, lam_q2, lam_k2, diff_norm, gmlp_ln_g, gmlp_ln_b, gmlp_ws, gmlp_bs, w_out, w_up, conv_w, conv_b, w_down):
    bp, tp, d = x_prompt.shape
    bs, ts, _ = x_sample.shape
    depth = w_in.shape[0]
    d_ff = w_down.shape[1]
    n_pool, page = cache_k.shape[1], cache_k.shape[2]
    past = page_table.shape[1] * page
    assert ts <= ROW_GROUP - (CONV_W - 1) and ts <= CHUNK_C
    assert tp % CHUNK_C == 0 and (bs * ROW_GROUP) % CHUNK_C == 0

    lb_soft = jax.nn.softmax(hgrn_lb.astype(F32), axis=0)
    lb_all = jnp.clip(jnp.cumsum(lb_soft, axis=0) - lb_soft[0], 0.0, 1.0 - 1e-6)

    tabs_p = _rope_tables(jnp.arange(tp))
    pos_s = jnp.minimum(jnp.arange(ROW_GROUP), ts - 1) + past
    tabs_s = tuple(jnp.tile(a, (bs, 1)) for a in _rope_tables(pos_s))

    ck = jnp.transpose(cache_k, (0, 1, 3, 4, 2)).reshape(depth, n_pool, D_QK, page)
    cv = cache_v.reshape(depth, n_pool, page * H_B, DV_B)
    n_grp = _pick(page_table.shape[1], (8, 4, 2))

    ns = bs * ROW_GROUP
    xp = x_prompt.reshape(bp * tp, d)
    xs = jnp.pad(x_sample, ((0, 0), (0, ROW_GROUP - ts), (0, 0))).reshape(ns, d)

    tm_p = _pick(tp, (512, 256, 128))
    tq = _pick(tp, (512, 256, 128))
    tn = _pick(d_ff, (256, 128))
    lc_p = _pick(tp, (CHUNK_A,))
    bs_p = _pick(lc_p, (16, ROW_GROUP))
    tb_p = _pick(tp, (4 * lc_p, lc_p))

    eye_g = jnp.eye(CHUNK_C // ROW_GROUP, dtype=F32)

    outs = {k: [] for k in ("k_p", "v_p", "k_s", "v_s", "hg_p", "hg_s", "gv_s", "cv_p", "cv_s")}
    for l in range(depth):
        lam_init = 0.8 - 0.6 * math.exp(-0.3 * l)
        lam = (jnp.exp(jnp.sum(lam_q1[l].astype(F32) * lam_k1[l].astype(F32)))
               - jnp.exp(jnp.sum(lam_q2[l].astype(F32) * lam_k2[l].astype(F32)))
               + lam_init).reshape(1, 1)
        lb = lb_all[l]
        lbp = jnp.stack([jnp.log(lb), jnp.log1p(-lb), 1.0 - lb])
        g_hgrn = jnp.tile(hgrn_norm[l], H_A)[None, :]
        g_diff = diff_norm[l][None, :]
        w_in_b = w_in[l].astype(BF16)
        w_out_b = w_out[l].astype(BF16)
        w_up_b = w_up[l].astype(BF16)
        w_down_b = w_down[l].astype(BF16)
        ws = gmlp_ws[l].astype(F32)
        bsl = gmlp_bs[l].astype(F32)
        bt_p = jnp.repeat(bsl.T, CG_C, axis=1)
        ws8 = jnp.pad(ws[:, :ts, :ts], ((0, 0), (0, ROW_GROUP - ts), (0, ROW_GROUP - ts)))
        ws_s = jnp.einsum('ab,gts->gatbs', eye_g, ws8).reshape(G_C, CHUNK_C, CHUNK_C)
        bt_s = jnp.tile(jnp.repeat(bsl.T[:ROW_GROUP], CG_C, axis=1), (CHUNK_C // ROW_GROUP, 1))
        lg = gmlp_ln_g[l][None, :]
        lbn = gmlp_ln_b[l][None, :]
        cb = conv_b[l][None, :]

        hg, q, k, v, uv = _proj(xp, norm_mix_pre[l][None, :], w_in_b, tabs_p, tm_p)
        oa, st = _hgrn(hg.reshape(bp, tp, -1), lbp, g_hgrn,
                       jnp.zeros((bp, D_A, D_A), F32), lc_p, bs_p, tb_p, lc_p)
        seq = lambda a: a.reshape(bp, tp, -1)
        ob = _attn_prompt(seq(q), seq(k), seq(v), lam, g_diff, 1.0 - lam_init, tq)
        oc, _ = _gmlp(uv, ws, bt_p, lg, lbn)
        x1 = _outproj(oa.reshape(-1, D_A), ob.reshape(-1, D_B), oc, w_out_b, xp,
                      norm_mix_post[l][None, :], tm_p)
        xp, tg, tv = _ffn(x1, norm_ffn_pre[l][None, :], w_up_b, conv_w[l], cb, w_down_b,
                          norm_ffn_post[l][None, :], None, tp, tn)
        outs["k_p"].append(k.reshape(bp, tp, 2 * H_B, DK_B))
        outs["v_p"].append(v.reshape(bp, tp, H_B, DV_B))
        outs["hg_p"].append(_state_from_bd(st))
        tail = jnp.concatenate([tg, tv], axis=1).reshape(bp, ROW_GROUP, 2 * d_ff)
        outs["cv_p"].append(tail[:, ROW_GROUP - (CONV_W - 1):])

        hg, q, k, v, uv = _proj(xs, norm_mix_pre[l][None, :], w_in_b, tabs_s, ns)
        oa, st = _hgrn(hg.reshape(bs, ROW_GROUP, -1), lbp, g_hgrn,
                       _state_to_bd(state_hgrn[l]), ROW_GROUP, ROW_GROUP, ROW_GROUP, ts)
        ob = _attn_decode(q, k, v, ck, cv, page_table, l, lam, g_diff, 1.0 - lam_init, ts, n_grp)
        oc, vn = _gmlp(uv, ws_s, bt_s, lg, lbn)
        x1 = _outproj(oa.reshape(-1, D_A), ob, oc, w_out_b, xs, norm_mix_post[l][None, :], ns)
        prev = jnp.pad(jnp.roll(state_conv[l].astype(F32), -1, axis=0),
                       ((0, 0), (ROW_GROUP - (CONV_W - 1), 0), (0, 0))).reshape(ns, 2 * d_ff)
        xs, tg, tv = _ffn(x1, norm_ffn_pre[l][None, :], w_up_b, conv_w[l], cb, w_down_b,
                          norm_ffn_post[l][None, :], prev, ns, tn)
        grp = lambda a: a.reshape(bs, ROW_GROUP, -1)[:, :ts]
        outs["k_s"].append(grp(k).reshape(bs, ts, 2 * H_B, DK_B))
        outs["v_s"].append(grp(v).reshape(bs, ts, H_B, DV_B))
        outs["hg_s"].append(_state_from_bd(st))
        outs["gv_s"].append(grp(vn))
        up_s = jnp.concatenate([tg, tv], axis=1).reshape(bs, ROW_GROUP, 2 * d_ff)
        outs["cv_s"].append(up_s[:, ts - (CONV_W - 1):ts])

    y_p = xp.reshape(bp, tp, d)
    y_s = xs.reshape(bs, ROW_GROUP, d)[:, :ts]
    st = lambda key: jnp.stack(outs[key])
    return (y_p, y_s, st("k_p"), st("v_p"), st("k_s"), st("v_s"), st("hg_p"), st("hg_s"),
            st("gv_s"), st("cv_p"), st("cv_s"))
```

```python
import functools
import math

import jax
import jax.numpy as jnp
from jax import lax
from jax.experimental import pallas as pl
from jax.experimental.pallas import tpu as pltpu

F32 = jnp.float32
BF16 = jnp.bfloat16

H_A = 4
DK_A = 64
D_A = H_A * DK_A
H_B = 4
DK_B = 64
DV_B = 128
D_QK = 2 * H_B * DK_B
D_B = H_B * DV_B
ROT_DIM = DK_B // 4
ROPE_THETA = 500000.0
G_C = 4
D_C = 256
CG_C = D_C // G_C
CHUNK_C = 128
CHUNK_A = 64
CONV_W = 3
ROW_GROUP = 8
LANES = 128
VMEM_LIMIT = 56 * 1024 * 1024
NEG_INF = float("-inf")


def _cparams(sem):
    return pltpu.CompilerParams(dimension_semantics=sem, vmem_limit_bytes=VMEM_LIMIT)


def _rms(x, g, eps=1e-6):
    return x * lax.rsqrt(jnp.mean(x * x, axis=-1, keepdims=True) + eps) * g


def _proj_kernel(x_ref, g_ref, w_ref, cos_ref, sa_ref, sb_ref, *out_refs, q_scale, attn_bf16):
    if attn_bf16:
        hg_ref, q_ref, k_ref, v_ref, kb_ref, vb_ref, uv_ref = out_refs
    else:
        hg_ref, q_ref, k_ref, v_ref, uv_ref = out_refs
    h = _rms(x_ref[...], g_ref[...]).astype(BF16)

    def mm(lo, hi):
        return jnp.dot(h, w_ref[:, lo:hi], preferred_element_type=F32)

    n_hg = 4 * D_A
    hg_ref[...] = mm(0, n_hg)
    cos = cos_ref[...]
    sa = sa_ref[...]
    sb = sb_ref[...]
    half = ROT_DIM // 2

    def rope(xx):
        return (xx * cos + pltpu.roll(xx, LANES - half, 1) * sa
                + pltpu.roll(xx, half, 1) * sb)

    first = lax.broadcasted_iota(jnp.int32, (1, LANES), 1) < DK_B
    for j in range(D_QK // LANES):
        lo = n_hg + j * LANES
        qj = rope(mm(lo, lo + LANES)) * q_scale
        lo = n_hg + D_QK + j * LANES
        kj = rope(mm(lo, lo + LANES))
        k_ref[:, j * LANES:(j + 1) * LANES] = kj
        if attn_bf16:
            q_ref[:, 2 * j * LANES:(2 * j + 1) * LANES] = jnp.where(first, qj, 0.0).astype(BF16)
            q_ref[:, (2 * j + 1) * LANES:(2 * j + 2) * LANES] = jnp.where(first, 0.0, qj).astype(BF16)
            kb_ref[:, j * LANES:(j + 1) * LANES] = kj.astype(BF16)
        else:
            q_ref[:, j * LANES:(j + 1) * LANES] = qj
    lo = n_hg + 2 * D_QK
    vv = mm(lo, lo + D_B)
    v_ref[...] = vv
    if attn_bf16:
        vb_ref[...] = vv.astype(BF16)
    uv_ref[...] = mm(lo + D_B, lo + D_B + 2 * D_C)


def _proj(x, g, w_in, tabs, tm, attn_bf16):
    n, d = x.shape
    n_in = w_in.shape[1]
    cos, sa, sb = tabs
    nt = cos.shape[0] // tm
    row = lambda i: (i, 0)
    fix = lambda i: (0, 0)
    tab = lambda i: (i % nt, 0)
    if attn_bf16:
        outs = [(4 * D_A, F32), (2 * D_QK, BF16), (D_QK, F32), (D_B, F32), (D_QK, BF16),
                (D_B, BF16), (2 * D_C, F32)]
    else:
        outs = [(4 * D_A, F32), (D_QK, F32), (D_QK, F32), (D_B, F32), (2 * D_C, F32)]
    return pl.pallas_call(
        functools.partial(_proj_kernel, q_scale=DK_B ** -0.5, attn_bf16=attn_bf16),
        grid=(n // tm,),
        in_specs=[pl.BlockSpec((tm, d), row), pl.BlockSpec((1, d), fix),
                  pl.BlockSpec((d, n_in), fix),
                  pl.BlockSpec((tm, LANES), tab), pl.BlockSpec((tm, LANES), tab),
                  pl.BlockSpec((tm, LANES), tab)],
        out_specs=[pl.BlockSpec((tm, w), row) for w, _ in outs],
        out_shape=[jax.ShapeDtypeStruct((n, w), dt) for w, dt in outs],
        compiler_params=_cparams(("parallel",)),
        name="proj",
    )(x, g, w_in, cos, sa, sb)


def _rope_tables(pos):
    half = ROT_DIM // 2
    inv = ROPE_THETA ** (-jnp.arange(half, dtype=F32) * (2.0 / ROT_DIM))
    ang = pos.astype(F32)[:, None] * inv[None, :]
    c, s = jnp.cos(ang), jnp.sin(ang)
    t = pos.shape[0]
    rest = DK_B - ROT_DIM
    cos64 = jnp.concatenate([c, c, jnp.ones((t, rest), F32)], axis=1)
    sa64 = jnp.concatenate([-s, jnp.zeros((t, half + rest), F32)], axis=1)
    sb64 = jnp.concatenate([jnp.zeros((t, half), F32), s, jnp.zeros((t, rest), F32)], axis=1)
    rep = LANES // DK_B
    return tuple(jnp.tile(a, (1, rep)) for a in (cos64, sa64, sb64))


def _hgrn_chunk(q, z, vi, ga, st, lb_ref, g_ref, cum_s, kk_s, v_s, *, lc, bs, t_valid):
    nblk = lc // bs
    log_sig = jnp.minimum(z, 0.0) - jnp.log1p(jnp.exp(-jnp.abs(z)))
    a = lb_ref[0:1, :]
    b = lb_ref[1:2, :] + log_sig
    log_f = jnp.maximum(a, b) + jnp.log1p(jnp.exp(-jnp.abs(a - b)))
    kk = lb_ref[2:3, :] * jax.nn.sigmoid(-z)
    rows = lax.broadcasted_iota(jnp.int32, (lc, 1), 0)
    if t_valid < lc:
        log_f = jnp.where(rows < t_valid, log_f, 0.0)
        kk = jnp.where(rows < t_valid, kk, 0.0)

    cum = log_f
    d = 1
    while d < lc:
        cum = cum + jnp.where(rows >= d, pltpu.roll(cum, d, 0), 0.0)
        d *= 2

    cum_s[...] = cum
    kk_s[...] = kk
    v_s[...] = vi

    hr = lax.broadcasted_iota(jnp.int32, (D_A, D_A), 0) // DK_A
    hc = lax.broadcasted_iota(jnp.int32, (D_A, D_A), 1) // DK_A
    same_head = hr == hc
    ones_bd = same_head.astype(BF16)

    q_dec = (q * jnp.exp(cum)).astype(BF16)
    o_inter = lax.dot_general(q_dec, st.astype(BF16), (((1,), (1,)), ((), ())),
                              preferred_element_type=F32)
    o_blk = [o_inter[i * bs:(i + 1) * bs] for i in range(nblk)]

    sub = ROW_GROUP
    for i in range(nblk):
        r0 = i * bs
        pieces = []
        for s in range(bs):
            lo = r0 + (s // sub) * sub
            cs = cum_s[r0 + s:r0 + s + 1, :]
            ks = kk_s[r0 + s:r0 + s + 1, :]
            dd = cum[lo:r0 + bs] - cs
            if s % sub:
                dd = jnp.where(rows[lo:r0 + bs] >= r0 + s, dd, NEG_INF)
            pieces.append(q[lo:r0 + bs] * jnp.exp(dd) * ks)
        p_all = jnp.dot(jnp.concatenate(pieces, axis=0).astype(BF16), ones_bd,
                        preferred_element_type=F32)
        off = 0
        acc = {}
        for s in range(bs):
            lo = (s // sub) * sub
            n = bs - lo
            contrib = p_all[off:off + n] * v_s[r0 + s:r0 + s + 1, :]
            acc[lo] = contrib if lo not in acc else acc[lo] + contrib
            off += n
        tot = acc[0]
        for lo, val in acc.items():
            if lo:
                tot = tot + jnp.concatenate([jnp.zeros((lo, D_A), F32), val], axis=0)
        o_blk[i] = o_blk[i] + tot

    if nblk > 1:
        hm = (lax.broadcasted_iota(jnp.int32, (H_A * bs, D_A), 0) // bs
              == lax.broadcasted_iota(jnp.int32, (H_A * bs, D_A), 1) // DK_A)
        for j in range(nblk - 1):
            r1 = (j + 1) * bs
            aj = cum[r1 - 1:r1, :]
            k_t = kk[r1 - bs:r1] * jnp.exp(aj - cum[r1 - bs:r1])
            q_t = (q[r1:] * jnp.exp(cum[r1:] - aj)).astype(BF16)
            k_bd = jnp.where(hm, jnp.concatenate([k_t] * H_A, axis=0), 0.0).astype(BF16)
            v_bd = jnp.where(hm, jnp.concatenate([vi[r1 - bs:r1]] * H_A, axis=0), 0.0).astype(BF16)
            s2 = lax.dot_general(q_t, k_bd, (((1,), (1,)), ((), ())),
                                 preferred_element_type=F32)
            o_off = jnp.dot(s2.astype(BF16), v_bd, preferred_element_type=F32)
            for i in range(j + 1, nblk):
                o_blk[i] = o_blk[i] + o_off[(i - j - 1) * bs:(i - j) * bs]

    last = cum[lc - 1:lc, :]
    k_dec = (kk * jnp.exp(last - cum)).astype(BF16)
    upd = lax.dot_general(vi.astype(BF16), k_dec, (((0,), (0,)), ((), ())),
                          preferred_element_type=F32)
    st_new = st * jnp.exp(last) + jnp.where(same_head, upd, 0.0)

    o = jnp.concatenate(o_blk, axis=0) if nblk > 1 else o_blk[0]
    sq = o * o
    hi = sq.astype(BF16)
    lo_ = (sq - hi.astype(F32)).astype(BF16)
    ms = (jnp.dot(hi, ones_bd, preferred_element_type=F32)
          + jnp.dot(lo_, ones_bd, preferred_element_type=F32)) * (1.0 / DK_A)
    oa = o * lax.rsqrt(ms + 1e-6) * g_ref[...] * (ga * jax.nn.sigmoid(ga))
    return oa, st_new


def _hgrn_kernel(hg_ref, lb_ref, g_ref, st0_ref, oa_ref, st_ref, st_s, cum_s, kk_s, v_s,
                 *, lc, bs, n_chunks, t_valid):
    t_idx = pl.program_id(1)

    @pl.when(t_idx == 0)
    def _():
        st_s[...] = st0_ref[0]

    def chunk(ci, carry):
        r = ci * lc if isinstance(ci, int) else pl.multiple_of(ci * lc, lc)
        blk = lambda k: hg_ref[0, pl.ds(r, lc), k * D_A:(k + 1) * D_A]
        oa, st_new = _hgrn_chunk(blk(0), blk(1), blk(2), blk(3), st_s[...], lb_ref, g_ref,
                                 cum_s, kk_s, v_s, lc=lc, bs=bs, t_valid=t_valid)
        oa_ref[0, pl.ds(r, lc), :] = oa
        st_s[...] = st_new
        return carry

    if n_chunks == 1:
        chunk(0, 0)
    else:
        lax.fori_loop(0, n_chunks, chunk, 0)

    @pl.when(t_idx == pl.num_programs(1) - 1)
    def _():
        st_ref[0] = st_s[...]


def _hgrn(hg, lbp, g, st0, lc, bs, tb, t_valid):
    b, t, w = hg.shape
    return pl.pallas_call(
        functools.partial(_hgrn_kernel, lc=lc, bs=bs, n_chunks=tb // lc, t_valid=t_valid),
        grid=(b, t // tb),
        in_specs=[pl.BlockSpec((1, tb, w), lambda i, c: (i, c, 0)),
                  pl.BlockSpec((3, D_A), lambda i, c: (0, 0)),
                  pl.BlockSpec((1, D_A), lambda i, c: (0, 0)),
                  pl.BlockSpec((1, D_A, D_A), lambda i, c: (i, 0, 0))],
        out_specs=[pl.BlockSpec((1, tb, D_A), lambda i, c: (i, c, 0)),
                   pl.BlockSpec((1, D_A, D_A), lambda i, c: (i, 0, 0))],
        out_shape=[jax.ShapeDtypeStruct((b, t, D_A), F32),
                   jax.ShapeDtypeStruct((b, D_A, D_A), F32)],
        scratch_shapes=[pltpu.VMEM((D_A, D_A), F32)] + [pltpu.VMEM((lc, D_A), F32)] * 3,
        compiler_params=_cparams(("parallel", "arbitrary")),
        name="hgrn",
    )(hg, lbp, g, st0)


def _state_to_bd(s):
    b = s.shape[0]
    st = jnp.swapaxes(s.astype(F32), 2, 3)
    eye = jnp.eye(H_A, dtype=F32)
    return (st[:, :, :, None, :] * eye[None, :, None, :, None]).reshape(b, D_A, D_A)


def _state_from_bd(st):
    b = st.shape[0]
    return jnp.einsum('bhvhc->bhcv', st.reshape(b, H_A, DK_A, H_A, DK_A))


def _attn_kernel(qt_ref, kt_ref, lam_ref, q1_ref, q2_ref, k_ref, v_ref, g_ref, o_ref,
                 m_s, l_s, acc_s, *, tq, out_scale):
    pidx = pl.program_id(2)
    qi = qt_ref[pidx]
    ki = kt_ref[pidx]

    @pl.when(ki == 0)
    def _():
        m_s[...] = jnp.full(m_s.shape, NEG_INF, F32)
        l_s[...] = jnp.zeros(l_s.shape, F32)
        acc_s[...] = jnp.zeros(acc_s.shape, F32)

    def step(masked):
        k = k_ref[0]
        v = v_ref[0]
        if masked:
            keep = (lax.broadcasted_iota(jnp.int32, (tq, tq), 0)
                    <= lax.broadcasted_iota(jnp.int32, (tq, tq), 1))
        for sub, q_ref in enumerate((q1_ref, q2_ref)):
            st = lax.dot_general(k, q_ref[0], (((1,), (1,)), ((), ())),
                                 preferred_element_type=F32)
            if masked:
                st = jnp.where(keep, st, NEG_INF)
            m_prev = m_s[sub]
            m_new = jnp.maximum(m_prev, jnp.max(st, axis=0, keepdims=True))
            alpha = jnp.exp(m_prev - m_new)
            p = jnp.exp(st - m_new)
            l_s[sub] = alpha * l_s[sub] + jnp.sum(p, axis=0, keepdims=True)
            pv = lax.dot_general(v, p.astype(BF16), (((0,), (0,)), ((), ())),
                                 preferred_element_type=F32)
            acc_s[sub] = alpha * acc_s[sub] + pv
            m_s[sub] = m_new

    @pl.when(ki < qi)
    def _():
        step(False)

    @pl.when(ki == qi)
    def _():
        step(True)
        lam = lam_ref[...]
        o = acc_s[0] * (1.0 / l_s[0]) - lam * (acc_s[1] * (1.0 / l_s[1]))
        o = o * lax.rsqrt(jnp.mean(o * o, axis=0, keepdims=True) + 1e-6)
        o_ref[0] = o.T * g_ref[...] * out_scale


def _attn_prompt(qq, kb, vb, lam, g, out_scale, tq):
    b, t, _ = kb.shape
    nq = t // tq
    pairs = [(a, c) for a in range(nq) for c in range(a + 1)]
    q_tab = jnp.asarray([p[0] for p in pairs], jnp.int32)
    k_tab = jnp.asarray([p[1] for p in pairs], jnp.int32)
    fix = lambda i, j, p, qt, kt: (0, 0)

    def qmap(sub):
        return lambda i, j, p, qt, kt: (i, qt[p], 2 * j + sub)

    kmap = lambda i, j, p, qt, kt: (i, kt[p], j)
    grid_spec = pltpu.PrefetchScalarGridSpec(
        num_scalar_prefetch=2,
        grid=(b, H_B, len(pairs)),
        in_specs=[pl.BlockSpec((1, 1), fix),
                  pl.BlockSpec((1, tq, LANES), qmap(0)),
                  pl.BlockSpec((1, tq, LANES), qmap(1)),
                  pl.BlockSpec((1, tq, LANES), kmap),
                  pl.BlockSpec((1, tq, LANES), kmap),
                  pl.BlockSpec((1, DV_B), fix)],
        out_specs=pl.BlockSpec((1, tq, LANES), lambda i, j, p, qt, kt: (i, qt[p], j)),
        scratch_shapes=[pltpu.VMEM((2, 1, tq), F32), pltpu.VMEM((2, 1, tq), F32),
                        pltpu.VMEM((2, DV_B, tq), F32)],
    )
    return pl.pallas_call(
        functools.partial(_attn_kernel, tq=tq, out_scale=out_scale),
        grid_spec=grid_spec,
        out_shape=jax.ShapeDtypeStruct((b, t, D_B), F32),
        compiler_params=_cparams(("parallel", "parallel", "arbitrary")),
        name="attn_prompt",
    )(q_tab, k_tab, lam, qq, qq, kb, vb, g)


def _decode_kernel(pt_ref, lam_ref, q_ref, kn_ref, vn_ref, *rest, n_grp, t_valid, out_scale):
    k_refs = rest[:n_grp]
    v_refs = rest[n_grp:2 * n_grp]
    g_ref, o_ref, qbd_s, m_s, l_s, acc_s = rest[2 * n_grp:]
    p_idx = pl.program_id(1)
    n_sub = 2 * H_B
    nr = n_sub * ROW_GROUP
    page = k_refs[0].shape[1]
    hrow = lax.broadcasted_iota(jnp.int32, (nr, 1), 0) // ROW_GROUP
    hcol = lax.broadcasted_iota(jnp.int32, (1, D_QK), 1) // DK_B

    @pl.when(p_idx == 0)
    def _():
        qt = jnp.concatenate([q_ref[...]] * n_sub, axis=0)
        qbd_s[...] = jnp.where(hrow == hcol, qt, 0.0).astype(BF16)
        m_s[...] = jnp.full(m_s.shape, NEG_INF, F32)
        l_s[...] = jnp.zeros(l_s.shape, F32)
        acc_s[...] = jnp.zeros(acc_s.shape, F32)

    def update(s, pv):
        m_prev = m_s[...]
        m_new = jnp.maximum(m_prev, jnp.max(s, axis=-1, keepdims=True))
        alpha = jnp.exp(m_prev - m_new)
        p = jnp.exp(s - m_new)
        l_s[...] = alpha * l_s[...] + jnp.sum(p, axis=-1, keepdims=True)
        acc_s[...] = alpha * acc_s[...] + pv(p.astype(BF16))
        m_s[...] = m_new

    qbd = qbd_s[...]
    s_pages = [jnp.dot(qbd, k_refs[i][...].astype(BF16), preferred_element_type=F32)
               for i in range(n_grp)]

    def pv_pages(p):
        tot = None
        for i in range(n_grp):
            vcat = jnp.concatenate(
                [v_refs[i][pl.ds(h, page, stride=H_B), :] for h in range(H_B)], axis=1)
            t = jnp.dot(p[:, i * page:(i + 1) * page], vcat.astype(BF16),
                        preferred_element_type=F32)
            tot = t if tot is None else tot + t
        return tot

    update(jnp.concatenate(s_pages, axis=1) if n_grp > 1 else s_pages[0], pv_pages)

    @pl.when(p_idx == pl.num_programs(1) - 1)
    def _():
        zpad = jnp.zeros((page - ROW_GROUP, D_QK), F32)
        kn = jnp.concatenate([kn_ref[...], zpad], axis=0).astype(BF16)
        vn = jnp.concatenate([vn_ref[...], zpad], axis=0).astype(BF16)
        r = lax.broadcasted_iota(jnp.int32, (nr, page), 0) % ROW_GROUP
        c = lax.broadcasted_iota(jnp.int32, (nr, page), 1)
        s_new = lax.dot_general(qbd, kn, (((1,), (1,)), ((), ())), preferred_element_type=F32)
        s_new = jnp.where((c <= r) & (c < t_valid), s_new, NEG_INF)
        update(s_new, lambda p: jnp.dot(p, vn, preferred_element_type=F32))
        lam = lam_ref[...]
        coef = jnp.where(hrow % 2 == 0, 1.0, -lam)
        vcol = lax.broadcasted_iota(jnp.int32, (1, D_B), 1) // DV_B
        contrib = jnp.where(vcol == hrow // 2, acc_s[...] / l_s[...] * coef, 0.0)
        o = contrib[0:ROW_GROUP]
        for h in range(1, n_sub):
            o = o + contrib[h * ROW_GROUP:(h + 1) * ROW_GROUP]
        for j in range(H_B):
            oj = o[:, j * DV_B:(j + 1) * DV_B]
            o_ref[:, j * DV_B:(j + 1) * DV_B] = _rms(oj, g_ref[...]) * out_scale


def _attn_decode(q, k_new, v_new, cache_kt, cache_v2, page_table, layer, lam, g, out_scale,
                 t_valid, n_grp):
    n = q.shape[0]
    nb, n_pages = page_table.shape
    page = cache_kt.shape[3]
    row = lambda i, p, pt: (i, 0)
    fix = lambda i, p, pt: (0, 0)

    def cmap(gi):
        return lambda i, p, pt: (layer, pt[i, p * n_grp + gi], 0, 0)

    nr = 2 * H_B * ROW_GROUP
    grid_spec = pltpu.PrefetchScalarGridSpec(
        num_scalar_prefetch=1,
        grid=(nb, n_pages // n_grp),
        in_specs=([pl.BlockSpec((1, 1), fix),
                   pl.BlockSpec((ROW_GROUP, D_QK), row),
                   pl.BlockSpec((ROW_GROUP, D_QK), row),
                   pl.BlockSpec((ROW_GROUP, D_B), row)]
                  + [pl.BlockSpec((None, None, D_QK, page), cmap(gi)) for gi in range(n_grp)]
                  + [pl.BlockSpec((None, None, page * H_B, DV_B), cmap(gi)) for gi in range(n_grp)]
                  + [pl.BlockSpec((1, DV_B), fix)]),
        out_specs=pl.BlockSpec((ROW_GROUP, D_B), row),
        scratch_shapes=[pltpu.VMEM((nr, D_QK), BF16), pltpu.VMEM((nr, 1), F32),
                        pltpu.VMEM((nr, 1), F32), pltpu.VMEM((nr, D_B), F32)],
    )
    return pl.pallas_call(
        functools.partial(_decode_kernel, n_grp=n_grp, t_valid=t_valid, out_scale=out_scale),
        grid_spec=grid_spec,
        out_shape=jax.ShapeDtypeStruct((n, D_B), F32),
        compiler_params=_cparams(("parallel", "arbitrary")),
        name="attn_decode",
    )(page_table, lam, q, k_new, v_new, *([cache_kt] * n_grp), *([cache_v2] * n_grp), g)


def _gmlp_kernel(uv_ref, w_ref, bt_ref, lg_ref, lb_ref, oc_ref, vn_ref, *, n_chunks):
    tr = lax.broadcasted_iota(jnp.int32, (CHUNK_C, CHUNK_C), 0)
    tc = lax.broadcasted_iota(jnp.int32, (CHUNK_C, CHUNK_C), 1)
    causal = tc <= tr
    grp = lax.broadcasted_iota(jnp.int32, (1, D_C), 1) // CG_C
    wgs = [jnp.where(causal, w_ref[gi], 0.0).astype(BF16) for gi in range(G_C)]
    for c in range(n_chunks):
        sl = pl.ds(c * CHUNK_C, CHUNK_C)
        u = uv_ref[sl, 0:D_C]
        v = uv_ref[sl, D_C:2 * D_C]
        mu = jnp.mean(v, axis=-1, keepdims=True)
        vc = v - mu
        var = jnp.mean(vc * vc, axis=-1, keepdims=True)
        vn = vc * lax.rsqrt(var + 1e-5) * lg_ref[...] + lb_ref[...]
        vn_ref[sl, :] = vn
        vnb = vn.astype(BF16)
        mixed = bt_ref[...]
        for gi in range(G_C):
            mg = jnp.dot(wgs[gi], vnb, preferred_element_type=F32)
            mixed = mixed + jnp.where(grp == gi, mg, 0.0)
        oc_ref[sl, :] = u * mixed


def _gmlp(uv, w, bt, lg, lb):
    n = uv.shape[0]
    n_chunks = _pick(n // CHUNK_C, (4, 2, 1))
    rows = n_chunks * CHUNK_C
    row = lambda i: (i, 0)
    fix = lambda i: (0, 0)
    return pl.pallas_call(
        functools.partial(_gmlp_kernel, n_chunks=n_chunks),
        grid=(n // rows,),
        in_specs=[pl.BlockSpec((rows, 2 * D_C), row),
                  pl.BlockSpec((G_C, CHUNK_C, CHUNK_C), lambda i: (0, 0, 0)),
                  pl.BlockSpec((CHUNK_C, D_C), fix),
                  pl.BlockSpec((1, D_C), fix), pl.BlockSpec((1, D_C), fix)],
        out_specs=[pl.BlockSpec((rows, D_C), row), pl.BlockSpec((rows, D_C), row)],
        out_shape=[jax.ShapeDtypeStruct((n, D_C), F32), jax.ShapeDtypeStruct((n, D_C), F32)],
        compiler_params=_cparams(("parallel",)),
        name="gmlp",
    )(uv, w, bt, lg, lb)


def _outproj_kernel(oa_ref, ob_ref, oc_ref, w_ref, x_ref, g_ref, o_ref):
    mix = jnp.dot(oa_ref[...].astype(BF16), w_ref[0:D_A, :], preferred_element_type=F32)
    mix += jnp.dot(ob_ref[...].astype(BF16), w_ref[D_A:D_A + D_B, :],
                   preferred_element_type=F32)
    mix += jnp.dot(oc_ref[...].astype(BF16), w_ref[D_A + D_B:, :],
                   preferred_element_type=F32)
    o_ref[...] = x_ref[...] + _rms(mix, g_ref[...])


def _outproj(oa, ob, oc, w_out, x, g, tm):
    n, d = x.shape
    row = lambda i: (i, 0)
    fix = lambda i: (0, 0)
    return pl.pallas_call(
        _outproj_kernel,
        grid=(n // tm,),
        in_specs=[pl.BlockSpec((tm, D_A), row), pl.BlockSpec((tm, D_B), row),
                  pl.BlockSpec((tm, D_C), row), pl.BlockSpec(w_out.shape, fix),
                  pl.BlockSpec((tm, d), row), pl.BlockSpec((1, d), fix)],
        out_specs=pl.BlockSpec((tm, d), row),
        out_shape=jax.ShapeDtypeStruct((n, d), F32),
        compiler_params=_cparams(("parallel",)),
        name="outproj",
    )(oa, ob, oc, w_out, x, g)


def _gelu_tanh(x):
    c = math.sqrt(2.0 / math.pi)
    return (0.5 * x) * (1.0 + jnp.tanh(x * (c + (c * 0.044715) * (x * x))))


def _ffn_kernel(*refs, rows, sub_rows, has_prev):
    if has_prev:
        (x_ref, gpre_ref, wg_ref, wv_ref, cwg_ref, cwv_ref, cbg_ref, cbv_ref, wd_ref, gpost_ref,
         pg_ref, pv_ref, y_ref, tg_ref, tv_ref, hn_s, acc_s) = refs
    else:
        (x_ref, gpre_ref, wg_ref, wv_ref, cwg_ref, cwv_ref, cbg_ref, cbv_ref, wd_ref, gpost_ref,
         y_ref, tg_ref, tv_ref, hn_s, acc_s) = refs
    j = pl.program_id(1)

    @pl.when(j == 0)
    def _():
        hn_s[...] = _rms(x_ref[...], gpre_ref[...]).astype(BF16)
        acc_s[...] = jnp.zeros(acc_s.shape, F32)

    tn = wg_ref.shape[1]
    n_sub = rows // sub_rows
    g8 = ROW_GROUP

    def branch(hn, w_ref, cw_ref, cb_ref, p_ref, t_ref, tail, last):
        up = jnp.dot(hn, w_ref[...], preferred_element_type=F32)
        if has_prev:
            ridx = lax.broadcasted_iota(jnp.int32, (rows, 1), 0)
            up = jnp.where(ridx % g8 >= g8 - (CONV_W - 1), p_ref[...], up)
            t_ref[...] = up
            sh1 = pltpu.roll(up, 1, 0)
            sh2 = pltpu.roll(up, 2, 0)
        else:
            if last:
                t_ref[...] = up[sub_rows - g8:, :]
            head = jnp.concatenate([tail, up[0:g8]], axis=0)
            sh1 = jnp.concatenate([pltpu.roll(head, 1, 0)[g8:], pltpu.roll(up, 1, 0)[g8:]], axis=0)
            sh2 = jnp.concatenate([pltpu.roll(head, 2, 0)[g8:], pltpu.roll(up, 2, 0)[g8:]], axis=0)
        y = cb_ref[...] + cw_ref[0:1, :] * sh2 + cw_ref[1:2, :] * sh1 + cw_ref[2:3, :] * up
        return y, up[sub_rows - g8:, :]

    tail_g = tail_v = jnp.zeros((g8, tn), F32)
    for r in range(n_sub):
        sl = pl.ds(r * sub_rows, sub_rows)
        hn = hn_s[sl, :]
        last = r == n_sub - 1
        yg, tail_g = branch(hn, wg_ref, cwg_ref, cbg_ref, pg_ref if has_prev else None, tg_ref,
                            tail_g, last)
        yv, tail_v = branch(hn, wv_ref, cwv_ref, cbv_ref, pv_ref if has_prev else None, tv_ref,
                            tail_v, last)
        hcur = (_gelu_tanh(yg) * yv).astype(BF16)
        acc_s[sl, :] += jnp.dot(hcur, wd_ref[...], preferred_element_type=F32)

    @pl.when(j == pl.num_programs(1) - 1)
    def _():
        y_ref[...] = x_ref[...] + _rms(acc_s[...], gpost_ref[...])


def _ffn(x, g_pre, w_up, conv_w, conv_b, w_down, g_post, prev, rows, sub_rows, tn):
    n, d = x.shape
    d_ff = w_down.shape[0]
    nff = d_ff // tn
    has_prev = prev is not None
    assert rows % sub_rows == 0 and (not has_prev or sub_rows == rows)
    xmap = lambda i, j: (i, 0)
    fix = lambda i, j: (0, 0)
    gate = lambda i, j: (0, j)
    val = lambda i, j: (0, j + nff)
    in_specs = [pl.BlockSpec((rows, d), xmap), pl.BlockSpec((1, d), fix),
                pl.BlockSpec((d, tn), gate), pl.BlockSpec((d, tn), val),
                pl.BlockSpec((CONV_W, tn), gate), pl.BlockSpec((CONV_W, tn), val),
                pl.BlockSpec((1, tn), gate), pl.BlockSpec((1, tn), val),
                pl.BlockSpec((tn, d), lambda i, j: (j, 0)), pl.BlockSpec((1, d), fix)]
    args = [x, g_pre, w_up, w_up, conv_w, conv_w, conv_b, conv_b, w_down, g_post]
    if has_prev:
        in_specs += [pl.BlockSpec((rows, tn), lambda i, j: (i, j)),
                     pl.BlockSpec((rows, tn), lambda i, j: (i, j + nff))]
        args += [prev, prev]
        t_rows = rows
    else:
        t_rows = ROW_GROUP
    t_spec = pl.BlockSpec((t_rows, tn), lambda i, j: (i, j))
    nb = n // rows
    return pl.pallas_call(
        functools.partial(_ffn_kernel, rows=rows, sub_rows=sub_rows, has_prev=has_prev),
        grid=(nb, nff),
        in_specs=in_specs,
        out_specs=[pl.BlockSpec((rows, d), xmap), t_spec, t_spec],
        out_shape=[jax.ShapeDtypeStruct((n, d), F32),
                   jax.ShapeDtypeStruct((nb * t_rows, d_ff), F32),
                   jax.ShapeDtypeStruct((nb * t_rows, d_ff), F32)],
        scratch_shapes=[pltpu.VMEM((rows, d), BF16), pltpu.VMEM((rows, d), F32)],
        compiler_params=_cparams(("parallel", "arbitrary")),
        name="ffn",
    )(*args)


def _pick(n, prefs):
    for p in prefs:
        if n % p == 0:
            return p
    return n


def kernel(x_prompt, x_sample, cache_k, cache_v, page_table, state_hgrn, state_conv, norm_mix_pre, norm_mix_post, norm_ffn_pre, norm_ffn_post, w_in, hgrn_lb, hgrn_norm, lam_q1, lam_k1, lam_q2, lam_k2, diff_norm, gmlp_ln_g, gmlp_ln_b, gmlp_ws, gmlp_bs, w_out, w_up, conv_w, conv_b, w_down):
    bp, tp, d = x_prompt.shape
    bs, ts, _ = x_sample.shape
    depth = w_in.shape[0]
    d_ff = w_down.shape[1]
    n_pool, page = cache_k.shape[1], cache_k.shape[2]
    past = page_table.shape[1] * page
    assert ts <= ROW_GROUP - (CONV_W - 1) and ts <= CHUNK_C
    assert tp % CHUNK_C == 0 and (bs * ROW_GROUP) % CHUNK_C == 0

    lb_soft = jax.nn.softmax(hgrn_lb.astype(F32), axis=0)
    lb_all = jnp.clip(jnp.cumsum(lb_soft, axis=0) - lb_soft[0], 0.0, 1.0 - 1e-6)

    tabs_p = _rope_tables(jnp.arange(tp))
    pos_s = jnp.minimum(jnp.arange(ROW_GROUP), ts - 1) + past
    tabs_s = tuple(jnp.tile(a, (bs, 1)) for a in _rope_tables(pos_s))

    ck = jnp.transpose(cache_k, (0, 1, 3, 4, 2)).reshape(depth, n_pool, D_QK, page)
    cv = cache_v.reshape(depth, n_pool, page * H_B, DV_B)
    n_grp = _pick(page_table.shape[1], (16, 8, 4, 2))

    ns = bs * ROW_GROUP
    xp = x_prompt.reshape(bp * tp, d)
    xs = jnp.pad(x_sample, ((0, 0), (0, ROW_GROUP - ts), (0, 0))).reshape(ns, d)

    tm_p = _pick(tp, (512, 256, 128))
    tq = _pick(tp, (512, 256, 128))
    tn = _pick(d_ff, (256, 128))
    lc_p = _pick(tp, (CHUNK_A,))
    bs_p = _pick(lc_p, (16, ROW_GROUP))
    tb_p = _pick(tp, (4 * lc_p, lc_p))
    sub_p = _pick(tp, (512, 256, 128))

    eye_g = jnp.eye(CHUNK_C // ROW_GROUP, dtype=F32)

    outs = {k: [] for k in ("k_p", "v_p", "k_s", "v_s", "hg_p", "hg_s", "gv_s", "cv_p", "cv_s")}
    for l in range(depth):
        lam_init = 0.8 - 0.6 * math.exp(-0.3 * l)
        lam = (jnp.exp(jnp.sum(lam_q1[l].astype(F32) * lam_k1[l].astype(F32)))
               - jnp.exp(jnp.sum(lam_q2[l].astype(F32) * lam_k2[l].astype(F32)))
               + lam_init).reshape(1, 1)
        lb = lb_all[l]
        lbp = jnp.stack([jnp.log(lb), jnp.log1p(-lb), 1.0 - lb])
        g_hgrn = jnp.tile(hgrn_norm[l], H_A)[None, :]
        g_diff = diff_norm[l][None, :]
        w_in_b = w_in[l].astype(BF16)
        w_out_b = w_out[l].astype(BF16)
        w_up_b = w_up[l].astype(BF16)
        w_down_b = w_down[l].astype(BF16)
        ws = gmlp_ws[l].astype(F32)
        bsl = gmlp_bs[l].astype(F32)
        bt_p = jnp.repeat(bsl.T, CG_C, axis=1)
        ws8 = jnp.pad(ws[:, :ts, :ts], ((0, 0), (0, ROW_GROUP - ts), (0, ROW_GROUP - ts)))
        ws_s = jnp.einsum('ab,gts->gatbs', eye_g, ws8).reshape(G_C, CHUNK_C, CHUNK_C)
        bt_s = jnp.tile(jnp.repeat(bsl.T[:ROW_GROUP], CG_C, axis=1), (CHUNK_C // ROW_GROUP, 1))
        lg = gmlp_ln_g[l][None, :]
        lbn = gmlp_ln_b[l][None, :]
        cb = conv_b[l][None, :]

        hg, qq, k, v, kb, vb, uv = _proj(xp, norm_mix_pre[l][None, :], w_in_b, tabs_p, tm_p, True)
        oa, st = _hgrn(hg.reshape(bp, tp, -1), lbp, g_hgrn,
                       jnp.zeros((bp, D_A, D_A), F32), lc_p, bs_p, tb_p, lc_p)
        seq = lambda a: a.reshape(bp, tp, -1)
        ob = _attn_prompt(seq(qq), seq(kb), seq(vb), lam, g_diff, 1.0 - lam_init, tq)
        oc, _ = _gmlp(uv, ws, bt_p, lg, lbn)
        x1 = _outproj(oa.reshape(-1, D_A), ob.reshape(-1, D_B), oc, w_out_b, xp,
                      norm_mix_post[l][None, :], tm_p)
        xp, tg, tv = _ffn(x1, norm_ffn_pre[l][None, :], w_up_b, conv_w[l], cb, w_down_b,
                          norm_ffn_post[l][None, :], None, tp, sub_p, tn)
        outs["k_p"].append(k.reshape(bp, tp, 2 * H_B, DK_B))
        outs["v_p"].append(v.reshape(bp, tp, H_B, DV_B))
        outs["hg_p"].append(_state_from_bd(st))
        tail = jnp.concatenate([tg, tv], axis=1).reshape(bp, ROW_GROUP, 2 * d_ff)
        outs["cv_p"].append(tail[:, ROW_GROUP - (CONV_W - 1):])

        hg, q, k, v, uv = _proj(xs, norm_mix_pre[l][None, :], w_in_b, tabs_s, ns, False)
        oa, st = _hgrn(hg.reshape(bs, ROW_GROUP, -1), lbp, g_hgrn,
                       _state_to_bd(state_hgrn[l]), ROW_GROUP, ROW_GROUP, ROW_GROUP, ts)
        ob = _attn_decode(q, k, v, ck, cv, page_table, l, lam, g_diff, 1.0 - lam_init, ts, n_grp)
        oc, vn = _gmlp(uv, ws_s, bt_s, lg, lbn)
        x1 = _outproj(oa.reshape(-1, D_A), ob, oc, w_out_b, xs, norm_mix_post[l][None, :], ns)
        prev = jnp.pad(jnp.roll(state_conv[l].astype(F32), -1, axis=0),
                       ((0, 0), (ROW_GROUP - (CONV_W - 1), 0), (0, 0))).reshape(ns, 2 * d_ff)
        xs, tg, tv = _ffn(x1, norm_ffn_pre[l][None, :], w_up_b, conv_w[l], cb, w_down_b,
                          norm_ffn_post[l][None, :], prev, ns, ns, tn)
        grp = lambda a: a.reshape(bs, ROW_GROUP, -1)[:, :ts]
        outs["k_s"].append(grp(k).reshape(bs, ts, 2 * H_B, DK_B))
        outs["v_s"].append(grp(v).reshape(bs, ts, H_B, DV_B))
        outs["hg_s"].append(_state_from_bd(st))
        outs["gv_s"].append(grp(vn))
        up_s = jnp.concatenate([tg, tv], axis=1).reshape(bs, ROW_GROUP, 2 * d_ff)
        outs["cv_s"].append(up_s[:, ts - (CONV_W - 1):ts])

    y_p = xp.reshape(bp, tp, d)
    y_s = xs.reshape(bs, ROW_GROUP, d)[:, :ts]
    st = lambda key: jnp.stack(outs[key])
    return (y_p, y_s, st("k_p"), st("v_p"), st("k_s"), st("v_s"), st("hg_p"), st("hg_s"),
            st("gv_s"), st("cv_p"), st("cv_s"))
```

```python
import functools
import math

import jax
import jax.numpy as jnp
from jax import lax
from jax.experimental import pallas as pl
from jax.experimental.pallas import tpu as pltpu

F32 = jnp.float32
BF16 = jnp.bfloat16

H_A = 4
DK_A = 64
D_A = H_A * DK_A
H_B = 4
DK_B = 64
DV_B = 128
D_QK = 2 * H_B * DK_B
D_B = H_B * DV_B
ROT_DIM = DK_B // 4
ROPE_THETA = 500000.0
G_C = 4
D_C = 256
CG_C = D_C // G_C
CHUNK_C = 128
CHUNK_A = 64
CONV_W = 3
ROW_GROUP = 8
LANES = 128
VMEM_LIMIT = 56 * 1024 * 1024
NEG_INF = float("-inf")


def _cparams(sem):
    return pltpu.CompilerParams(dimension_semantics=sem, vmem_limit_bytes=VMEM_LIMIT)


def _rms(x, g, eps=1e-6):
    return x * lax.rsqrt(jnp.mean(x * x, axis=-1, keepdims=True) + eps) * g


def _proj_kernel(x_ref, g_ref, w_ref, cos_ref, sa_ref, sb_ref, *out_refs, q_scale, attn_bf16):
    if attn_bf16:
        hg_ref, q_ref, k_ref, v_ref, kb_ref, vb_ref, uv_ref = out_refs
    else:
        hg_ref, q_ref, k_ref, v_ref, uv_ref = out_refs
    h = _rms(x_ref[...], g_ref[...]).astype(BF16)

    def mm(lo, hi):
        return jnp.dot(h, w_ref[:, lo:hi], preferred_element_type=F32)

    n_hg = 4 * D_A
    hg_ref[...] = mm(0, n_hg)
    cos = cos_ref[...]
    sa = sa_ref[...]
    sb = sb_ref[...]
    half = ROT_DIM // 2

    def rope(xx):
        return (xx * cos + pltpu.roll(xx, LANES - half, 1) * sa
                + pltpu.roll(xx, half, 1) * sb)

    first = lax.broadcasted_iota(jnp.int32, (1, LANES), 1) < DK_B
    for j in range(D_QK // LANES):
        lo = n_hg + j * LANES
        qj = rope(mm(lo, lo + LANES)) * q_scale
        lo = n_hg + D_QK + j * LANES
        kj = rope(mm(lo, lo + LANES))
        if attn_bf16:
            k_ref[j * LANES:(j + 1) * LANES, :] = kj.T
            q_ref[:, 2 * j * LANES:(2 * j + 1) * LANES] = jnp.where(first, qj, 0.0).astype(BF16)
            q_ref[:, (2 * j + 1) * LANES:(2 * j + 2) * LANES] = jnp.where(first, 0.0, qj).astype(BF16)
            kb_ref[:, j * LANES:(j + 1) * LANES] = kj.astype(BF16)
        else:
            k_ref[:, j * LANES:(j + 1) * LANES] = kj
            q_ref[:, j * LANES:(j + 1) * LANES] = qj
    lo = n_hg + 2 * D_QK
    vv = mm(lo, lo + D_B)
    if attn_bf16:
        for hh in range(H_B):
            v_ref[:, hh, :] = vv[:, hh * DV_B:(hh + 1) * DV_B]
        vb_ref[...] = vv.astype(BF16)
    else:
        v_ref[...] = vv
    uv_ref[...] = mm(lo + D_B, lo + D_B + 2 * D_C)


def _proj(x, g, w_in, layer, tabs, tm, attn_bf16):
    n, d = x.shape
    n_in = w_in.shape[2]
    cos, sa, sb = tabs
    nt = cos.shape[0] // tm
    row = lambda i: (i, 0)
    fix = lambda i: (0, 0)
    tab = lambda i: (i % nt, 0)
    if attn_bf16:
        outs = [(4 * D_A, F32), (2 * D_QK, BF16), None, None, (D_QK, BF16),
                (D_B, BF16), (2 * D_C, F32)]
    else:
        outs = [(4 * D_A, F32), (D_QK, F32), (D_QK, F32), (D_B, F32), (2 * D_C, F32)]
    out_specs = [o and pl.BlockSpec((tm, o[0]), row) for o in outs]
    out_shape = [o and jax.ShapeDtypeStruct((n, o[0]), o[1]) for o in outs]
    if attn_bf16:
        t_len = cos.shape[0]
        out_specs[2] = pl.BlockSpec((None, D_QK, tm), lambda i: (i // nt, 0, i % nt))
        out_shape[2] = jax.ShapeDtypeStruct((n // t_len, D_QK, t_len), F32)
        out_specs[3] = pl.BlockSpec((tm, H_B, DV_B), lambda i: (i, 0, 0))
        out_shape[3] = jax.ShapeDtypeStruct((n, H_B, DV_B), F32)
    return pl.pallas_call(
        functools.partial(_proj_kernel, q_scale=DK_B ** -0.5, attn_bf16=attn_bf16),
        grid=(n // tm,),
        in_specs=[pl.BlockSpec((tm, d), row), pl.BlockSpec((1, d), fix),
                  pl.BlockSpec((None, d, n_in), lambda i: (layer, 0, 0)),
                  pl.BlockSpec((tm, LANES), tab), pl.BlockSpec((tm, LANES), tab),
                  pl.BlockSpec((tm, LANES), tab)],
        out_specs=out_specs,
        out_shape=out_shape,
        compiler_params=_cparams(("parallel",)),
        name="proj",
    )(x, g, w_in, cos, sa, sb)


def _rope_tables(pos):
    half = ROT_DIM // 2
    inv = ROPE_THETA ** (-jnp.arange(half, dtype=F32) * (2.0 / ROT_DIM))
    ang = pos.astype(F32)[:, None] * inv[None, :]
    c, s = jnp.cos(ang), jnp.sin(ang)
    t = pos.shape[0]
    rest = DK_B - ROT_DIM
    cos64 = jnp.concatenate([c, c, jnp.ones((t, rest), F32)], axis=1)
    sa64 = jnp.concatenate([-s, jnp.zeros((t, half + rest), F32)], axis=1)
    sb64 = jnp.concatenate([jnp.zeros((t, half), F32), s, jnp.zeros((t, rest), F32)], axis=1)
    rep = LANES // DK_B
    return tuple(jnp.tile(a, (1, rep)) for a in (cos64, sa64, sb64))


def _hgrn_chunk(q, z, vi, ga, st, lb_ref, g_ref, cum_s, kk_s, v_s, *, lc, bs, t_valid):
    nblk = lc // bs
    log_sig = jnp.minimum(z, 0.0) - jnp.log1p(jnp.exp(-jnp.abs(z)))
    a = lb_ref[0:1, :]
    b = lb_ref[1:2, :] + log_sig
    log_f = jnp.maximum(a, b) + jnp.log1p(jnp.exp(-jnp.abs(a - b)))
    kk = lb_ref[2:3, :] * jax.nn.sigmoid(-z)
    rows = lax.broadcasted_iota(jnp.int32, (lc, 1), 0)
    if t_valid < lc:
        log_f = jnp.where(rows < t_valid, log_f, 0.0)
        kk = jnp.where(rows < t_valid, kk, 0.0)

    cum = log_f
    d = 1
    while d < lc:
        cum = cum + jnp.where(rows >= d, pltpu.roll(cum, d, 0), 0.0)
        d *= 2

    cum_s[...] = cum
    kk_s[...] = kk
    v_s[...] = vi

    hr = lax.broadcasted_iota(jnp.int32, (D_A, D_A), 0) // DK_A
    hc = lax.broadcasted_iota(jnp.int32, (D_A, D_A), 1) // DK_A
    same_head = hr == hc
    ones_bd = same_head.astype(BF16)

    q_dec = (q * jnp.exp(cum)).astype(BF16)
    o_inter = lax.dot_general(q_dec, st.astype(BF16), (((1,), (1,)), ((), ())),
                              preferred_element_type=F32)
    o_blk = [o_inter[i * bs:(i + 1) * bs] for i in range(nblk)]

    sub = ROW_GROUP
    for i in range(nblk):
        r0 = i * bs
        pieces = []
        for s in range(bs):
            lo = r0 + (s // sub) * sub
            cs = cum_s[r0 + s:r0 + s + 1, :]
            ks = kk_s[r0 + s:r0 + s + 1, :]
            dd = cum[lo:r0 + bs] - cs
            if s % sub:
                dd = jnp.where(rows[lo:r0 + bs] >= r0 + s, dd, NEG_INF)
            pieces.append(q[lo:r0 + bs] * jnp.exp(dd) * ks)
        p_all = jnp.dot(jnp.concatenate(pieces, axis=0).astype(BF16), ones_bd,
                        preferred_element_type=F32)
        off = 0
        acc = {}
        for s in range(bs):
            lo = (s // sub) * sub
            n = bs - lo
            contrib = p_all[off:off + n] * v_s[r0 + s:r0 + s + 1, :]
            acc[lo] = contrib if lo not in acc else acc[lo] + contrib
            off += n
        tot = acc[0]
        for lo, val in acc.items():
            if lo:
                tot = tot + jnp.concatenate([jnp.zeros((lo, D_A), F32), val], axis=0)
        o_blk[i] = o_blk[i] + tot

    if nblk > 1:
        hm = (lax.broadcasted_iota(jnp.int32, (H_A * bs, D_A), 0) // bs
              == lax.broadcasted_iota(jnp.int32, (H_A * bs, D_A), 1) // DK_A)
        for j in range(nblk - 1):
            r1 = (j + 1) * bs
            aj = cum[r1 - 1:r1, :]
            k_t = kk[r1 - bs:r1] * jnp.exp(aj - cum[r1 - bs:r1])
            q_t = (q[r1:] * jnp.exp(cum[r1:] - aj)).astype(BF16)
            k_bd = jnp.where(hm, jnp.concatenate([k_t] * H_A, axis=0), 0.0).astype(BF16)
            v_bd = jnp.where(hm, jnp.concatenate([vi[r1 - bs:r1]] * H_A, axis=0), 0.0).astype(BF16)
            s2 = lax.dot_general(q_t, k_bd, (((1,), (1,)), ((), ())),
                                 preferred_element_type=F32)
            o_off = jnp.dot(s2.astype(BF16), v_bd, preferred_element_type=F32)
            for i in range(j + 1, nblk):
                o_blk[i] = o_blk[i] + o_off[(i - j - 1) * bs:(i - j) * bs]

    last = cum[lc - 1:lc, :]
    k_dec = (kk * jnp.exp(last - cum)).astype(BF16)
    upd = lax.dot_general(vi.astype(BF16), k_dec, (((0,), (0,)), ((), ())),
                          preferred_element_type=F32)
    st_new = st * jnp.exp(last) + jnp.where(same_head, upd, 0.0)

    o = jnp.concatenate(o_blk, axis=0) if nblk > 1 else o_blk[0]
    sq = o * o
    hi = sq.astype(BF16)
    lo_ = (sq - hi.astype(F32)).astype(BF16)
    ms = (jnp.dot(hi, ones_bd, preferred_element_type=F32)
          + jnp.dot(lo_, ones_bd, preferred_element_type=F32)) * (1.0 / DK_A)
    oa = o * lax.rsqrt(ms + 1e-6) * g_ref[...] * (ga * jax.nn.sigmoid(ga))
    return oa, st_new


def _hgrn_kernel(hg_ref, lb_ref, g_ref, st0_ref, oa_ref, st_ref, st_s, cum_s, kk_s, v_s,
                 *, lc, bs, n_chunks, t_valid):
    t_idx = pl.program_id(1)

    @pl.when(t_idx == 0)
    def _():
        st_s[...] = st0_ref[0]

    def chunk(ci, carry):
        r = ci * lc if isinstance(ci, int) else pl.multiple_of(ci * lc, lc)
        blk = lambda k: hg_ref[0, pl.ds(r, lc), k * D_A:(k + 1) * D_A]
        oa, st_new = _hgrn_chunk(blk(0), blk(1), blk(2), blk(3), st_s[...], lb_ref, g_ref,
                                 cum_s, kk_s, v_s, lc=lc, bs=bs, t_valid=t_valid)
        oa_ref[0, pl.ds(r, lc), :] = oa
        st_s[...] = st_new
        return carry

    for ci in range(n_chunks):
        chunk(ci, 0)

    @pl.when(t_idx == pl.num_programs(1) - 1)
    def _():
        st_ref[0] = st_s[...]


def _hgrn(hg, lbp, g, st0, lc, bs, tb, t_valid):
    b, t, w = hg.shape
    return pl.pallas_call(
        functools.partial(_hgrn_kernel, lc=lc, bs=bs, n_chunks=tb // lc, t_valid=t_valid),
        grid=(b, t // tb),
        in_specs=[pl.BlockSpec((1, tb, w), lambda i, c: (i, c, 0)),
                  pl.BlockSpec((3, D_A), lambda i, c: (0, 0)),
                  pl.BlockSpec((1, D_A), lambda i, c: (0, 0)),
                  pl.BlockSpec((1, D_A, D_A), lambda i, c: (i, 0, 0))],
        out_specs=[pl.BlockSpec((1, tb, D_A), lambda i, c: (i, c, 0)),
                   pl.BlockSpec((1, D_A, D_A), lambda i, c: (i, 0, 0))],
        out_shape=[jax.ShapeDtypeStruct((b, t, D_A), F32),
                   jax.ShapeDtypeStruct((b, D_A, D_A), F32)],
        scratch_shapes=[pltpu.VMEM((D_A, D_A), F32)] + [pltpu.VMEM((lc, D_A), F32)] * 3,
        compiler_params=_cparams(("parallel", "arbitrary")),
        name="hgrn",
    )(hg, lbp, g, st0)


def _state_to_bd(s):
    b = s.shape[0]
    st = jnp.swapaxes(s.astype(F32), 2, 3)
    eye = jnp.eye(H_A, dtype=F32)
    return (st[:, :, :, None, :] * eye[None, :, None, :, None]).reshape(b, D_A, D_A)


def _state_from_bd(st):
    b = st.shape[0]
    return jnp.einsum('bhvhc->bhcv', st.reshape(b, H_A, DK_A, H_A, DK_A))


def _attn_kernel(lam_ref, q1_ref, q2_ref, k_ref, v_ref, g_ref, o_ref, *, tq, out_scale):
    t = k_ref.shape[1]
    nq = t // tq
    keep = (lax.broadcasted_iota(jnp.int32, (tq, tq), 0)
            <= lax.broadcasted_iota(jnp.int32, (tq, tq), 1))
    lam = lam_ref[...]
    for qi in range(nq):
        qsl = pl.ds(qi * tq, tq)
        heads = []
        for q_ref in (q1_ref, q2_ref):
            q = q_ref[0, qsl, :]
            m = jnp.full((1, tq), NEG_INF, F32)
            l = jnp.zeros((1, tq), F32)
            acc = jnp.zeros((DV_B, tq), F32)
            for ki in range(qi + 1):
                ksl = pl.ds(ki * tq, tq)
                st = lax.dot_general(k_ref[0, ksl, :], q, (((1,), (1,)), ((), ())),
                                     preferred_element_type=F32)
                if ki == qi:
                    st = jnp.where(keep, st, NEG_INF)
                m_new = jnp.maximum(m, jnp.max(st, axis=0, keepdims=True))
                alpha = jnp.exp(m - m_new)
                p = jnp.exp(st - m_new)
                l = alpha * l + jnp.sum(p, axis=0, keepdims=True)
                pv = lax.dot_general(v_ref[0, ksl, :], p.astype(BF16), (((0,), (0,)), ((), ())),
                                     preferred_element_type=F32)
                acc = alpha * acc + pv
                m = m_new
            heads.append(acc * (1.0 / l))
        o = heads[0] - lam * heads[1]
        o = o * lax.rsqrt(jnp.mean(o * o, axis=0, keepdims=True) + 1e-6)
        o_ref[0, qsl, :] = o.T * g_ref[...] * out_scale


def _attn_prompt(qq, kb, vb, lam, g, out_scale, tq):
    b, t, _ = kb.shape
    fix = lambda i, j: (0, 0)

    def qmap(sub):
        return lambda i, j: (i, 0, 2 * j + sub)

    kmap = lambda i, j: (i, 0, j)
    return pl.pallas_call(
        functools.partial(_attn_kernel, tq=tq, out_scale=out_scale),
        grid=(b, H_B),
        in_specs=[pl.BlockSpec((1, 1), fix),
                  pl.BlockSpec((1, t, LANES), qmap(0)),
                  pl.BlockSpec((1, t, LANES), qmap(1)),
                  pl.BlockSpec((1, t, LANES), kmap),
                  pl.BlockSpec((1, t, LANES), kmap),
                  pl.BlockSpec((1, DV_B), fix)],
        out_specs=pl.BlockSpec((1, t, LANES), kmap),
        out_shape=jax.ShapeDtypeStruct((b, t, D_B), F32),
        compiler_params=_cparams(("parallel", "parallel")),
        name="attn_prompt",
    )(lam, qq, qq, kb, vb, g)


def _decode_kernel(pt_ref, lam_ref, q_ref, kn_ref, vn_ref, *rest, n_grp, t_valid, out_scale):
    k_refs = rest[:n_grp]
    v_refs = rest[n_grp:2 * n_grp]
    g_ref, o_ref, qbd_s, m_s, l_s, acc_s = rest[2 * n_grp:]
    p_idx = pl.program_id(1)
    n_sub = 2 * H_B
    nr = n_sub * ROW_GROUP
    page = k_refs[0].shape[1]
    hrow = lax.broadcasted_iota(jnp.int32, (nr, 1), 0) // ROW_GROUP
    hcol = lax.broadcasted_iota(jnp.int32, (1, D_QK), 1) // DK_B

    @pl.when(p_idx == 0)
    def _():
        qt = jnp.concatenate([q_ref[...]] * n_sub, axis=0)
        qbd_s[...] = jnp.where(hrow == hcol, qt, 0.0).astype(BF16)
        m_s[...] = jnp.full(m_s.shape, NEG_INF, F32)
        l_s[...] = jnp.zeros(l_s.shape, F32)
        acc_s[...] = jnp.zeros(acc_s.shape, F32)

    def update(s, pv):
        m_prev = m_s[...]
        m_new = jnp.maximum(m_prev, jnp.max(s, axis=-1, keepdims=True))
        alpha = jnp.exp(m_prev - m_new)
        p = jnp.exp(s - m_new)
        l_s[...] = alpha * l_s[...] + jnp.sum(p, axis=-1, keepdims=True)
        acc_s[...] = alpha * acc_s[...] + pv(p.astype(BF16))
        m_s[...] = m_new

    qbd = qbd_s[...]
    s_pages = [jnp.dot(qbd, k_refs[i][...].astype(BF16), preferred_element_type=F32)
               for i in range(n_grp)]

    def pv_pages(p):
        tot = None
        for i in range(n_grp):
            vcat = jnp.concatenate(
                [v_refs[i][pl.ds(h, page, stride=H_B), :] for h in range(H_B)], axis=1)
            t = jnp.dot(p[:, i * page:(i + 1) * page], vcat.astype(BF16),
                        preferred_element_type=F32)
            tot = t if tot is None else tot + t
        return tot

    update(jnp.concatenate(s_pages, axis=1) if n_grp > 1 else s_pages[0], pv_pages)

    @pl.when(p_idx == pl.num_programs(1) - 1)
    def _():
        zpad = jnp.zeros((page - ROW_GROUP, D_QK), F32)
        kn = jnp.concatenate([kn_ref[...], zpad], axis=0).astype(BF16)
        vn = jnp.concatenate([vn_ref[...], zpad], axis=0).astype(BF16)
        r = lax.broadcasted_iota(jnp.int32, (nr, page), 0) % ROW_GROUP
        c = lax.broadcasted_iota(jnp.int32, (nr, page), 1)
        s_new = lax.dot_general(qbd, kn, (((1,), (1,)), ((), ())), preferred_element_type=F32)
        s_new = jnp.where((c <= r) & (c < t_valid), s_new, NEG_INF)
        update(s_new, lambda p: jnp.dot(p, vn, preferred_element_type=F32))
        lam = lam_ref[...]
        coef = jnp.where(hrow % 2 == 0, 1.0, -lam)
        vcol = lax.broadcasted_iota(jnp.int32, (1, D_B), 1) // DV_B
        contrib = jnp.where(vcol == hrow // 2, acc_s[...] / l_s[...] * coef, 0.0)
        o = contrib[0:ROW_GROUP]
        for h in range(1, n_sub):
            o = o + contrib[h * ROW_GROUP:(h + 1) * ROW_GROUP]
        for j in range(H_B):
            oj = o[:, j * DV_B:(j + 1) * DV_B]
            o_ref[:, j * DV_B:(j + 1) * DV_B] = _rms(oj, g_ref[...]) * out_scale


def _attn_decode(q, k_new, v_new, cache_kt, cache_v2, page_table, layer, lam, g, out_scale,
                 t_valid, n_grp):
    n = q.shape[0]
    nb, n_pages = page_table.shape
    page = cache_kt.shape[3]
    row = lambda i, p, pt: (i, 0)
    fix = lambda i, p, pt: (0, 0)

    def cmap(gi):
        return lambda i, p, pt: (layer, pt[i, p * n_grp + gi], 0, 0)

    nr = 2 * H_B * ROW_GROUP
    grid_spec = pltpu.PrefetchScalarGridSpec(
        num_scalar_prefetch=1,
        grid=(nb, n_pages // n_grp),
        in_specs=([pl.BlockSpec((1, 1), fix),
                   pl.BlockSpec((ROW_GROUP, D_QK), row),
                   pl.BlockSpec((ROW_GROUP, D_QK), row),
                   pl.BlockSpec((ROW_GROUP, D_B), row)]
                  + [pl.BlockSpec((None, None, D_QK, page), cmap(gi)) for gi in range(n_grp)]
                  + [pl.BlockSpec((None, None, page * H_B, DV_B), cmap(gi)) for gi in range(n_grp)]
                  + [pl.BlockSpec((1, DV_B), fix)]),
        out_specs=pl.BlockSpec((ROW_GROUP, D_B), row),
        scratch_shapes=[pltpu.VMEM((nr, D_QK), BF16), pltpu.VMEM((nr, 1), F32),
                        pltpu.VMEM((nr, 1), F32), pltpu.VMEM((nr, D_B), F32)],
    )
    return pl.pallas_call(
        functools.partial(_decode_kernel, n_grp=n_grp, t_valid=t_valid, out_scale=out_scale),
        grid_spec=grid_spec,
        out_shape=jax.ShapeDtypeStruct((n, D_B), F32),
        compiler_params=_cparams(("parallel", "arbitrary")),
        name="attn_decode",
    )(page_table, lam, q, k_new, v_new, *([cache_kt] * n_grp), *([cache_v2] * n_grp), g)


def _gmlp_kernel(uv_ref, w_ref, bt_ref, lg_ref, lb_ref, oc_ref, vn_ref, *, n_chunks):
    tr = lax.broadcasted_iota(jnp.int32, (CHUNK_C, CHUNK_C), 0)
    tc = lax.broadcasted_iota(jnp.int32, (CHUNK_C, CHUNK_C), 1)
    causal = tc <= tr
    grp = lax.broadcasted_iota(jnp.int32, (1, D_C), 1) // CG_C
    wgs = [jnp.where(causal, w_ref[gi], 0.0).astype(BF16) for gi in range(G_C)]
    for c in range(n_chunks):
        sl = pl.ds(c * CHUNK_C, CHUNK_C)
        u = uv_ref[sl, 0:D_C]
        v = uv_ref[sl, D_C:2 * D_C]
        mu = jnp.mean(v, axis=-1, keepdims=True)
        vc = v - mu
        var = jnp.mean(vc * vc, axis=-1, keepdims=True)
        vn = vc * lax.rsqrt(var + 1e-5) * lg_ref[...] + lb_ref[...]
        vn_ref[sl, :] = vn
        vnb = vn.astype(BF16)
        mixed = bt_ref[...]
        for gi in range(G_C):
            mg = jnp.dot(wgs[gi], vnb, preferred_element_type=F32)
            mixed = mixed + jnp.where(grp == gi, mg, 0.0)
        oc_ref[sl, :] = u * mixed


def _gmlp(uv, w, bt, lg, lb):
    n = uv.shape[0]
    n_chunks = _pick(n // CHUNK_C, (4, 2, 1))
    rows = n_chunks * CHUNK_C
    row = lambda i: (i, 0)
    fix = lambda i: (0, 0)
    return pl.pallas_call(
        functools.partial(_gmlp_kernel, n_chunks=n_chunks),
        grid=(n // rows,),
        in_specs=[pl.BlockSpec((rows, 2 * D_C), row),
                  pl.BlockSpec((G_C, CHUNK_C, CHUNK_C), lambda i: (0, 0, 0)),
                  pl.BlockSpec((CHUNK_C, D_C), fix),
                  pl.BlockSpec((1, D_C), fix), pl.BlockSpec((1, D_C), fix)],
        out_specs=[pl.BlockSpec((rows, D_C), row), pl.BlockSpec((rows, D_C), row)],
        out_shape=[jax.ShapeDtypeStruct((n, D_C), F32), jax.ShapeDtypeStruct((n, D_C), F32)],
        compiler_params=_cparams(("parallel",)),
        name="gmlp",
    )(uv, w, bt, lg, lb)


def _outproj_kernel(oa_ref, ob_ref, oc_ref, w_ref, x_ref, g_ref, o_ref):
    mix = jnp.dot(oa_ref[...].astype(BF16), w_ref[0:D_A, :], preferred_element_type=F32)
    mix += jnp.dot(ob_ref[...].astype(BF16), w_ref[D_A:D_A + D_B, :],
                   preferred_element_type=F32)
    mix += jnp.dot(oc_ref[...].astype(BF16), w_ref[D_A + D_B:, :],
                   preferred_element_type=F32)
    o_ref[...] = x_ref[...] + _rms(mix, g_ref[...])


def _outproj(oa, ob, oc, w_out, layer, x, g, tm):
    n, d = x.shape
    row = lambda i: (i, 0)
    fix = lambda i: (0, 0)
    return pl.pallas_call(
        _outproj_kernel,
        grid=(n // tm,),
        in_specs=[pl.BlockSpec((tm, D_A), row), pl.BlockSpec((tm, D_B), row),
                  pl.BlockSpec((tm, D_C), row),
                  pl.BlockSpec((None,) + w_out.shape[1:], lambda i: (layer, 0, 0)),
                  pl.BlockSpec((tm, d), row), pl.BlockSpec((1, d), fix)],
        out_specs=pl.BlockSpec((tm, d), row),
        out_shape=jax.ShapeDtypeStruct((n, d), F32),
        compiler_params=_cparams(("parallel",)),
        name="outproj",
    )(oa, ob, oc, w_out, x, g)


def _gelu_tanh(x):
    c = math.sqrt(2.0 / math.pi)
    return (0.5 * x) * (1.0 + jnp.tanh(x * (c + (c * 0.044715) * (x * x))))


def _ffn_kernel(*refs, rows, sub_rows, has_prev):
    if has_prev:
        (x_ref, gpre_ref, wg_ref, wv_ref, cwg_ref, cwv_ref, cbg_ref, cbv_ref, wd_ref, gpost_ref,
         pg_ref, pv_ref, y_ref, tg_ref, tv_ref, hn_s, acc_s) = refs
    else:
        (x_ref, gpre_ref, wg_ref, wv_ref, cwg_ref, cwv_ref, cbg_ref, cbv_ref, wd_ref, gpost_ref,
         y_ref, tg_ref, tv_ref, hn_s, acc_s) = refs
    j = pl.program_id(1)

    @pl.when(j == 0)
    def _():
        hn_s[...] = _rms(x_ref[...], gpre_ref[...]).astype(BF16)
        acc_s[...] = jnp.zeros(acc_s.shape, F32)

    tn = wg_ref.shape[1]
    n_sub = rows // sub_rows
    g8 = ROW_GROUP

    def branch(hn, w_ref, cw_ref, cb_ref, p_ref, t_ref, tail, last):
        up = jnp.dot(hn, w_ref[...], preferred_element_type=F32)
        if has_prev:
            ridx = lax.broadcasted_iota(jnp.int32, (rows, 1), 0)
            up = jnp.where(ridx % g8 >= g8 - (CONV_W - 1), p_ref[...], up)
            t_ref[...] = up
            sh1 = pltpu.roll(up, 1, 0)
            sh2 = pltpu.roll(up, 2, 0)
        else:
            if last:
                t_ref[...] = up[sub_rows - g8:, :]
            head = jnp.concatenate([tail, up[0:g8]], axis=0)
            sh1 = jnp.concatenate([pltpu.roll(head, 1, 0)[g8:], pltpu.roll(up, 1, 0)[g8:]], axis=0)
            sh2 = jnp.concatenate([pltpu.roll(head, 2, 0)[g8:], pltpu.roll(up, 2, 0)[g8:]], axis=0)
        y = cb_ref[...] + cw_ref[0:1, :] * sh2 + cw_ref[1:2, :] * sh1 + cw_ref[2:3, :] * up
        return y, up[sub_rows - g8:, :]

    tail_g = tail_v = jnp.zeros((g8, tn), F32)
    for r in range(n_sub):
        sl = pl.ds(r * sub_rows, sub_rows)
        hn = hn_s[sl, :]
        last = r == n_sub - 1
        yg, tail_g = branch(hn, wg_ref, cwg_ref, cbg_ref, pg_ref if has_prev else None, tg_ref,
                            tail_g, last)
        yv, tail_v = branch(hn, wv_ref, cwv_ref, cbv_ref, pv_ref if has_prev else None, tv_ref,
                            tail_v, last)
        hcur = (_gelu_tanh(yg) * yv).astype(BF16)
        acc_s[sl, :] += jnp.dot(hcur, wd_ref[...], preferred_element_type=F32)

    @pl.when(j == pl.num_programs(1) - 1)
    def _():
        y_ref[...] = x_ref[...] + _rms(acc_s[...], gpost_ref[...])


def _ffn(x, g_pre, w_up, conv_w, conv_b, w_down, layer, g_post, prev, rows, sub_rows, tn):
    n, d = x.shape
    d_ff = w_down.shape[1]
    nff = d_ff // tn
    has_prev = prev is not None
    assert rows % sub_rows == 0 and (not has_prev or sub_rows == rows)
    xmap = lambda i, j: (i, 0)
    fix = lambda i, j: (0, 0)
    gate = lambda i, j: (0, j)
    val = lambda i, j: (0, j + nff)
    in_specs = [pl.BlockSpec((rows, d), xmap), pl.BlockSpec((1, d), fix),
                pl.BlockSpec((None, d, tn), lambda i, j: (layer, 0, j)),
                pl.BlockSpec((None, d, tn), lambda i, j: (layer, 0, j + nff)),
                pl.BlockSpec((CONV_W, tn), gate), pl.BlockSpec((CONV_W, tn), val),
                pl.BlockSpec((1, tn), gate), pl.BlockSpec((1, tn), val),
                pl.BlockSpec((None, tn, d), lambda i, j: (layer, j, 0)), pl.BlockSpec((1, d), fix)]
    args = [x, g_pre, w_up, w_up, conv_w, conv_w, conv_b, conv_b, w_down, g_post]
    if has_prev:
        in_specs += [pl.BlockSpec((rows, tn), lambda i, j: (i, j)),
                     pl.BlockSpec((rows, tn), lambda i, j: (i, j + nff))]
        args += [prev, prev]
        t_rows = rows
    else:
        t_rows = ROW_GROUP
    t_spec = pl.BlockSpec((t_rows, tn), lambda i, j: (i, j))
    nb = n // rows
    return pl.pallas_call(
        functools.partial(_ffn_kernel, rows=rows, sub_rows=sub_rows, has_prev=has_prev),
        grid=(nb, nff),
        in_specs=in_specs,
        out_specs=[pl.BlockSpec((rows, d), xmap), t_spec, t_spec],
        out_shape=[jax.ShapeDtypeStruct((n, d), F32),
                   jax.ShapeDtypeStruct((nb * t_rows, d_ff), F32),
                   jax.ShapeDtypeStruct((nb * t_rows, d_ff), F32)],
        scratch_shapes=[pltpu.VMEM((rows, d), BF16), pltpu.VMEM((rows, d), F32)],
        compiler_params=_cparams(("parallel", "arbitrary")),
        name="ffn",
    )(*args)


def _pick(n, prefs):
    for p in prefs:
        if n % p == 0:
            return p
    return n


def kernel(x_prompt, x_sample, cache_k, cache_v, page_table, state_hgrn, state_conv, norm_mix_pre, norm_mix_post, norm_ffn_pre, norm_ffn_post, w_in, hgrn_lb, hgrn_norm, lam_q1, lam_k1, lam_q2, lam_k2, diff_norm, gmlp_ln_g, gmlp_ln_b, gmlp_ws, gmlp_bs, w_out, w_up, conv_w, conv_b, w_down):
    bp, tp, d = x_prompt.shape
    bs, ts, _ = x_sample.shape
    depth = w_in.shape[0]
    d_ff = w_down.shape[1]
    n_pool, page = cache_k.shape[1], cache_k.shape[2]
    past = page_table.shape[1] * page
    assert ts <= ROW_GROUP - (CONV_W - 1) and ts <= CHUNK_C
    assert tp % CHUNK_C == 0 and (bs * ROW_GROUP) % CHUNK_C == 0

    lb_soft = jax.nn.softmax(hgrn_lb.astype(F32), axis=0)
    lb_all = jnp.clip(jnp.cumsum(lb_soft, axis=0) - lb_soft[0], 0.0, 1.0 - 1e-6)

    tabs_p = _rope_tables(jnp.arange(tp))
    pos_s = jnp.minimum(jnp.arange(ROW_GROUP), ts - 1) + past
    tabs_s = tuple(jnp.tile(a, (bs, 1)) for a in _rope_tables(pos_s))

    ck = jnp.transpose(cache_k, (0, 1, 3, 4, 2)).reshape(depth, n_pool, D_QK, page)
    cv = cache_v.reshape(depth, n_pool, page * H_B, DV_B)
    n_grp = _pick(page_table.shape[1], (16, 8, 4, 2))

    ns = bs * ROW_GROUP
    xp = x_prompt.reshape(bp * tp, d)
    xs = jnp.pad(x_sample, ((0, 0), (0, ROW_GROUP - ts), (0, 0))).reshape(ns, d)

    tm_p = _pick(tp, (512, 256, 128))
    tq = _pick(tp, (512, 256, 128))
    tn = _pick(d_ff, (256, 128))
    lc_p = _pick(tp, (CHUNK_A,))
    bs_p = _pick(lc_p, (16, ROW_GROUP))
    tb_p = _pick(tp, (4 * lc_p, lc_p))
    sub_p = _pick(tp, (512, 256, 128))

    eye_g = jnp.eye(CHUNK_C // ROW_GROUP, dtype=F32)

    w_in_b, w_out_b, w_up_b, w_down_b = (w.astype(BF16) for w in (w_in, w_out, w_up, w_down))

    outs = {k: [] for k in ("k_p", "v_p", "k_s", "v_s", "hg_p", "hg_s", "gv_s", "cv_p", "cv_s")}
    for l in range(depth):
        lam_init = 0.8 - 0.6 * math.exp(-0.3 * l)
        lam = (jnp.exp(jnp.sum(lam_q1[l].astype(F32) * lam_k1[l].astype(F32)))
               - jnp.exp(jnp.sum(lam_q2[l].astype(F32) * lam_k2[l].astype(F32)))
               + lam_init).reshape(1, 1)
        lb = lb_all[l]
        lbp = jnp.stack([jnp.log(lb), jnp.log1p(-lb), 1.0 - lb])
        g_hgrn = jnp.tile(hgrn_norm[l], H_A)[None, :]
        g_diff = diff_norm[l][None, :]
        ws = gmlp_ws[l].astype(F32)
        bsl = gmlp_bs[l].astype(F32)
        bt_p = jnp.repeat(bsl.T, CG_C, axis=1)
        ws8 = jnp.pad(ws[:, :ts, :ts], ((0, 0), (0, ROW_GROUP - ts), (0, ROW_GROUP - ts)))
        ws_s = jnp.einsum('ab,gts->gatbs', eye_g, ws8).reshape(G_C, CHUNK_C, CHUNK_C)
        bt_s = jnp.tile(jnp.repeat(bsl.T[:ROW_GROUP], CG_C, axis=1), (CHUNK_C // ROW_GROUP, 1))
        lg = gmlp_ln_g[l][None, :]
        lbn = gmlp_ln_b[l][None, :]
        cb = conv_b[l][None, :]

        hg, qq, k, v, kb, vb, uv = _proj(xp, norm_mix_pre[l][None, :], w_in_b, l, tabs_p, tm_p, True)
        oa, st = _hgrn(hg.reshape(bp, tp, -1), lbp, g_hgrn,
                       jnp.zeros((bp, D_A, D_A), F32), lc_p, bs_p, tb_p, lc_p)
        seq = lambda a: a.reshape(bp, tp, -1)
        ob = _attn_prompt(seq(qq), seq(kb), seq(vb), lam, g_diff, 1.0 - lam_init, tq)
        oc, _ = _gmlp(uv, ws, bt_p, lg, lbn)
        x1 = _outproj(oa.reshape(-1, D_A), ob.reshape(-1, D_B), oc, w_out_b, l, xp,
                      norm_mix_post[l][None, :], tm_p)
        xp, tg, tv = _ffn(x1, norm_ffn_pre[l][None, :], w_up_b, conv_w[l], cb, w_down_b, l,
                          norm_ffn_post[l][None, :], None, tp, sub_p, tn)
        outs["k_p"].append(jnp.transpose(k.reshape(bp, 2 * H_B, DK_B, tp), (0, 3, 1, 2)))
        outs["v_p"].append(v.reshape(bp, tp, H_B, DV_B))
        outs["hg_p"].append(_state_from_bd(st))
        tail = jnp.concatenate([tg, tv], axis=1).reshape(bp, ROW_GROUP, 2 * d_ff)
        outs["cv_p"].append(tail[:, ROW_GROUP - (CONV_W - 1):])

        hg, q, k, v, uv = _proj(xs, norm_mix_pre[l][None, :], w_in_b, l, tabs_s, ns, False)
        oa, st = _hgrn(hg.reshape(bs, ROW_GROUP, -1), lbp, g_hgrn,
                       _state_to_bd(state_hgrn[l]), ROW_GROUP, ROW_GROUP, ROW_GROUP, ts)
        ob = _attn_decode(q, k, v, ck, cv, page_table, l, lam, g_diff, 1.0 - lam_init, ts, n_grp)
        oc, vn = _gmlp(uv, ws_s, bt_s, lg, lbn)
        x1 = _outproj(oa.reshape(-1, D_A), ob, oc, w_out_b, l, xs, norm_mix_post[l][None, :], ns)
        prev = jnp.pad(jnp.roll(state_conv[l].astype(F32), -1, axis=0),
                       ((0, 0), (ROW_GROUP - (CONV_W - 1), 0), (0, 0))).reshape(ns, 2 * d_ff)
        xs, tg, tv = _ffn(x1, norm_ffn_pre[l][None, :], w_up_b, conv_w[l], cb, w_down_b, l,
                          norm_ffn_post[l][None, :], prev, ns, ns, tn)
        grp = lambda a: a.reshape(bs, ROW_GROUP, -1)[:, :ts]
        outs["k_s"].append(grp(k).reshape(bs, ts, 2 * H_B, DK_B))
        outs["v_s"].append(grp(v).reshape(bs, ts, H_B, DV_B))
        outs["hg_s"].append(_state_from_bd(st))
        outs["gv_s"].append(grp(vn))
        up_s = jnp.concatenate([tg, tv], axis=1).reshape(bs, ROW_GROUP, 2 * d_ff)
        outs["cv_s"].append(up_s[:, ts - (CONV_W - 1):ts])

    y_p = xp.reshape(bp, tp, d)
    y_s = xs.reshape(bs, ROW_GROUP, d)[:, :ts]
    st = lambda key: jnp.stack(outs[key])
    return (y_p, y_s, st("k_p"), st("v_p"), st("k_s"), st("v_s"), st("hg_p"), st("hg_s"),
            st("gv_s"), st("cv_p"), st("cv_s"))
```

```python
import functools
import math

import jax
import jax.numpy as jnp
from jax import lax
from jax.experimental import pallas as pl
from jax.experimental.pallas import tpu as pltpu

F32 = jnp.float32
BF16 = jnp.bfloat16

H_A = 4
DK_A = 64
D_A = H_A * DK_A
H_B = 4
DK_B = 64
DV_B = 128
D_QK = 2 * H_B * DK_B
D_B = H_B * DV_B
ROT_DIM = DK_B // 4
ROPE_THETA = 500000.0
G_C = 4
D_C = 256
CG_C = D_C // G_C
CHUNK_C = 128
CHUNK_A = 64
CONV_W = 3
ROW_GROUP = 8
LANES = 128
VMEM_LIMIT = 56 * 1024 * 1024
NEG_INF = float("-inf")


def _cparams(sem):
    return pltpu.CompilerParams(dimension_semantics=sem, vmem_limit_bytes=VMEM_LIMIT)


def _rms(x, g, eps=1e-6):
    return x * lax.rsqrt(jnp.mean(x * x, axis=-1, keepdims=True) + eps) * g


def _proj_kernel(x_ref, g_ref, w_ref, cos_ref, sa_ref, sb_ref, *out_refs, q_scale, attn_bf16):
    if attn_bf16:
        hg_ref, q_ref, k_ref, v_ref, kb_ref, vb_ref, uv_ref = out_refs
    else:
        hg_ref, q_ref, k_ref, v_ref, uv_ref = out_refs
    h = _rms(x_ref[...], g_ref[...]).astype(BF16)

    def mm(lo, hi):
        return jnp.dot(h, w_ref[:, lo:hi], preferred_element_type=F32)

    n_hg = 4 * D_A
    hg_ref[...] = mm(0, n_hg)
    cos = cos_ref[...]
    sa = sa_ref[...]
    sb = sb_ref[...]
    half = ROT_DIM // 2

    def rope(xx):
        return (xx * cos + pltpu.roll(xx, LANES - half, 1) * sa
                + pltpu.roll(xx, half, 1) * sb)

    first = lax.broadcasted_iota(jnp.int32, (1, LANES), 1) < DK_B
    for j in range(D_QK // LANES):
        lo = n_hg + j * LANES
        qj = rope(mm(lo, lo + LANES)) * q_scale
        lo = n_hg + D_QK + j * LANES
        kj = rope(mm(lo, lo + LANES))
        if attn_bf16:
            k_ref[j * LANES:(j + 1) * LANES, :] = kj.T
            q_ref[:, 2 * j * LANES:(2 * j + 1) * LANES] = jnp.where(first, qj, 0.0).astype(BF16)
            q_ref[:, (2 * j + 1) * LANES:(2 * j + 2) * LANES] = jnp.where(first, 0.0, qj).astype(BF16)
            kb_ref[:, j * LANES:(j + 1) * LANES] = kj.astype(BF16)
        else:
            k_ref[:, j * LANES:(j + 1) * LANES] = kj
            q_ref[:, j * LANES:(j + 1) * LANES] = qj
    lo = n_hg + 2 * D_QK
    vv = mm(lo, lo + D_B)
    if attn_bf16:
        for hh in range(H_B):
            v_ref[:, hh, :] = vv[:, hh * DV_B:(hh + 1) * DV_B]
        vb_ref[...] = vv.astype(BF16)
    else:
        v_ref[...] = vv
    uv_ref[...] = mm(lo + D_B, lo + D_B + 2 * D_C)


def _proj(x, g, w_in, layer, tabs, tm, attn_bf16):
    n, d = x.shape
    n_in = w_in.shape[2]
    cos, sa, sb = tabs
    nt = cos.shape[0] // tm
    row = lambda i: (i, 0)
    fix = lambda i: (0, 0)
    tab = lambda i: (i % nt, 0)
    if attn_bf16:
        outs = [(4 * D_A, F32), (2 * D_QK, BF16), None, None, (D_QK, BF16),
                (D_B, BF16), (2 * D_C, F32)]
    else:
        outs = [(4 * D_A, F32), (D_QK, F32), (D_QK, F32), (D_B, F32), (2 * D_C, F32)]
    out_specs = [o and pl.BlockSpec((tm, o[0]), row) for o in outs]
    out_shape = [o and jax.ShapeDtypeStruct((n, o[0]), o[1]) for o in outs]
    if attn_bf16:
        t_len = cos.shape[0]
        out_specs[2] = pl.BlockSpec((None, D_QK, tm), lambda i: (i // nt, 0, i % nt))
        out_shape[2] = jax.ShapeDtypeStruct((n // t_len, D_QK, t_len), F32)
        out_specs[3] = pl.BlockSpec((tm, H_B, DV_B), lambda i: (i, 0, 0))
        out_shape[3] = jax.ShapeDtypeStruct((n, H_B, DV_B), F32)
    return pl.pallas_call(
        functools.partial(_proj_kernel, q_scale=DK_B ** -0.5, attn_bf16=attn_bf16),
        grid=(n // tm,),
        in_specs=[pl.BlockSpec((tm, d), row), pl.BlockSpec((1, d), fix),
                  pl.BlockSpec((None, d, n_in), lambda i: (layer, 0, 0)),
                  pl.BlockSpec((tm, LANES), tab), pl.BlockSpec((tm, LANES), tab),
                  pl.BlockSpec((tm, LANES), tab)],
        out_specs=out_specs,
        out_shape=out_shape,
        compiler_params=_cparams(("parallel",)),
        name="proj",
    )(x, g, w_in, cos, sa, sb)


def _rope_tables(pos):
    half = ROT_DIM // 2
    inv = ROPE_THETA ** (-jnp.arange(half, dtype=F32) * (2.0 / ROT_DIM))
    ang = pos.astype(F32)[:, None] * inv[None, :]
    c, s = jnp.cos(ang), jnp.sin(ang)
    t = pos.shape[0]
    rest = DK_B - ROT_DIM
    cos64 = jnp.concatenate([c, c, jnp.ones((t, rest), F32)], axis=1)
    sa64 = jnp.concatenate([-s, jnp.zeros((t, half + rest), F32)], axis=1)
    sb64 = jnp.concatenate([jnp.zeros((t, half), F32), s, jnp.zeros((t, rest), F32)], axis=1)
    rep = LANES // DK_B
    return tuple(jnp.tile(a, (1, rep)) for a in (cos64, sa64, sb64))


def _hgrn_chunk(q, z, vi, ga, st, lb_ref, g_ref, cum_s, kk_s, v_s, *, lc, bs, t_valid):
    nblk = lc // bs
    log_sig = jnp.minimum(z, 0.0) - jnp.log1p(jnp.exp(-jnp.abs(z)))
    a = lb_ref[0:1, :]
    b = lb_ref[1:2, :] + log_sig
    log_f = jnp.maximum(a, b) + jnp.log1p(jnp.exp(-jnp.abs(a - b)))
    kk = lb_ref[2:3, :] * jax.nn.sigmoid(-z)
    rows = lax.broadcasted_iota(jnp.int32, (lc, 1), 0)
    if t_valid < lc:
        log_f = jnp.where(rows < t_valid, log_f, 0.0)
        kk = jnp.where(rows < t_valid, kk, 0.0)

    cum = log_f
    d = 1
    while d < lc:
        cum = cum + jnp.where(rows >= d, pltpu.roll(cum, d, 0), 0.0)
        d *= 2

    cum_s[...] = cum
    kk_s[...] = kk
    v_s[...] = vi

    hr = lax.broadcasted_iota(jnp.int32, (D_A, D_A), 0) // DK_A
    hc = lax.broadcasted_iota(jnp.int32, (D_A, D_A), 1) // DK_A
    same_head = hr == hc
    ones_bd = same_head.astype(BF16)

    q_dec = (q * jnp.exp(cum)).astype(BF16)
    o_inter = lax.dot_general(q_dec, st.astype(BF16), (((1,), (1,)), ((), ())),
                              preferred_element_type=F32)
    o_blk = [o_inter[i * bs:(i + 1) * bs] for i in range(nblk)]

    sub = ROW_GROUP
    for i in range(nblk):
        r0 = i * bs
        pieces = []
        for s in range(bs):
            lo = r0 + (s // sub) * sub
            cs = cum_s[r0 + s:r0 + s + 1, :]
            ks = kk_s[r0 + s:r0 + s + 1, :]
            dd = cum[lo:r0 + bs] - cs
            if s % sub:
                dd = jnp.where(rows[lo:r0 + bs] >= r0 + s, dd, NEG_INF)
            pieces.append(q[lo:r0 + bs] * jnp.exp(dd) * ks)
        p_all = jnp.dot(jnp.concatenate(pieces, axis=0).astype(BF16), ones_bd,
                        preferred_element_type=F32)
        off = 0
        acc = {}
        for s in range(bs):
            lo = (s // sub) * sub
            n = bs - lo
            contrib = p_all[off:off + n] * v_s[r0 + s:r0 + s + 1, :]
            acc[lo] = contrib if lo not in acc else acc[lo] + contrib
            off += n
        tot = acc[0]
        for lo, val in acc.items():
            if lo:
                tot = tot + jnp.concatenate([jnp.zeros((lo, D_A), F32), val], axis=0)
        o_blk[i] = o_blk[i] + tot

    if nblk > 1:
        hm = (lax.broadcasted_iota(jnp.int32, (H_A * bs, D_A), 0) // bs
              == lax.broadcasted_iota(jnp.int32, (H_A * bs, D_A), 1) // DK_A)
        for j in range(nblk - 1):
            r1 = (j + 1) * bs
            aj = cum[r1 - 1:r1, :]
            k_t = kk[r1 - bs:r1] * jnp.exp(aj - cum[r1 - bs:r1])
            q_t = (q[r1:] * jnp.exp(cum[r1:] - aj)).astype(BF16)
            k_bd = jnp.where(hm, jnp.concatenate([k_t] * H_A, axis=0), 0.0).astype(BF16)
            v_bd = jnp.where(hm, jnp.concatenate([vi[r1 - bs:r1]] * H_A, axis=0), 0.0).astype(BF16)
            s2 = lax.dot_general(q_t, k_bd, (((1,), (1,)), ((), ())),
                                 preferred_element_type=F32)
            o_off = jnp.dot(s2.astype(BF16), v_bd, preferred_element_type=F32)
            for i in range(j + 1, nblk):
                o_blk[i] = o_blk[i] + o_off[(i - j - 1) * bs:(i - j) * bs]

    last = cum[lc - 1:lc, :]
    k_dec = (kk * jnp.exp(last - cum)).astype(BF16)
    upd = lax.dot_general(vi.astype(BF16), k_dec, (((0,), (0,)), ((), ())),
                          preferred_element_type=F32)
    st_new = st * jnp.exp(last) + jnp.where(same_head, upd, 0.0)

    o = jnp.concatenate(o_blk, axis=0) if nblk > 1 else o_blk[0]
    sq = o * o
    hi = sq.astype(BF16)
    lo_ = (sq - hi.astype(F32)).astype(BF16)
    ms = (jnp.dot(hi, ones_bd, preferred_element_type=F32)
          + jnp.dot(lo_, ones_bd, preferred_element_type=F32)) * (1.0 / DK_A)
    oa = o * lax.rsqrt(ms + 1e-6) * g_ref[...] * (ga * jax.nn.sigmoid(ga))
    return oa, st_new


def _hgrn_kernel(*refs, lc, bs, n_chunks, n_seq, t_valid, zero_init):
    if zero_init:
        hg_ref, lb_ref, g_ref, oa_ref, st_ref, st_s, cum_s, kk_s, v_s = refs
    else:
        hg_ref, lb_ref, g_ref, st0_ref, oa_ref, st_ref, st_s, cum_s, kk_s, v_s = refs
    t_idx = pl.program_id(1)

    @pl.when(t_idx == 0)
    def _():
        if zero_init:
            st_s[...] = jnp.zeros(st_s.shape, F32)
        else:
            same_head = (lax.broadcasted_iota(jnp.int32, (D_A, D_A), 0) // DK_A
                         == lax.broadcasted_iota(jnp.int32, (D_A, D_A), 1) // DK_A)
            for i in range(n_seq):
                x4 = jnp.concatenate([st0_ref[i]] * H_A, axis=1)
                st_s[i] = jnp.where(same_head, x4, 0.0).T

    for i in range(n_seq):
        for ci in range(n_chunks):
            rows = pl.ds(ci * lc, lc)
            blk = lambda k: hg_ref[i, rows, k * D_A:(k + 1) * D_A]
            oa, st_new = _hgrn_chunk(blk(0), blk(1), blk(2), blk(3), st_s[i], lb_ref, g_ref,
                                     cum_s.at[i], kk_s.at[i], v_s.at[i],
                                     lc=lc, bs=bs, t_valid=t_valid)
            oa_ref[i, rows, :] = oa
            st_s[i] = st_new

    @pl.when(t_idx == pl.num_programs(1) - 1)
    def _():
        for i in range(n_seq):
            bd = st_s[i].T
            tot = bd[:, 0:DK_A]
            for h in range(1, H_A):
                tot = tot + bd[:, h * DK_A:(h + 1) * DK_A]
            st_ref[i] = tot


def _hgrn(hg, lbp, g, st0, lc, bs, tb, n_seq, t_valid):
    b, t, w = hg.shape
    zero_init = st0 is None
    smap = lambda i, c: (i, 0, 0)
    in_specs = [pl.BlockSpec((n_seq, tb, w), lambda i, c: (i, c, 0)),
                pl.BlockSpec((3, D_A), lambda i, c: (0, 0)),
                pl.BlockSpec((1, D_A), lambda i, c: (0, 0))]
    args = [hg, lbp, g]
    if not zero_init:
        in_specs.append(pl.BlockSpec((n_seq, D_A, DK_A), smap))
        args.append(st0.astype(F32).reshape(b, D_A, DK_A))
    oa, st = pl.pallas_call(
        functools.partial(_hgrn_kernel, lc=lc, bs=bs, n_chunks=tb // lc, n_seq=n_seq,
                          t_valid=t_valid, zero_init=zero_init),
        grid=(b // n_seq, t // tb),
        in_specs=in_specs,
        out_specs=[pl.BlockSpec((n_seq, tb, D_A), lambda i, c: (i, c, 0)),
                   pl.BlockSpec((n_seq, D_A, DK_A), smap)],
        out_shape=[jax.ShapeDtypeStruct((b, t, D_A), F32),
                   jax.ShapeDtypeStruct((b, D_A, DK_A), F32)],
        scratch_shapes=([pltpu.VMEM((n_seq, D_A, D_A), F32)]
                        + [pltpu.VMEM((n_seq, lc, D_A), F32)] * 3),
        compiler_params=_cparams(("parallel", "arbitrary")),
        name="hgrn",
    )(*args)
    return oa, st.reshape(b, H_A, DK_A, DK_A)


def _attn_kernel(lam_ref, q1_ref, q2_ref, k_ref, v_ref, g_ref, o_ref, *, tq, out_scale):
    t = k_ref.shape[1]
    nq = t // tq
    keep = (lax.broadcasted_iota(jnp.int32, (tq, tq), 0)
            <= lax.broadcasted_iota(jnp.int32, (tq, tq), 1))
    lam = lam_ref[...]
    for qi in range(nq):
        qsl = pl.ds(qi * tq, tq)
        heads = []
        for q_ref in (q1_ref, q2_ref):
            q = q_ref[0, qsl, :]
            m = jnp.full((1, tq), NEG_INF, F32)
            l = jnp.zeros((1, tq), F32)
            acc = jnp.zeros((DV_B, tq), F32)
            for ki in range(qi + 1):
                ksl = pl.ds(ki * tq, tq)
                st = lax.dot_general(k_ref[0, ksl, :], q, (((1,), (1,)), ((), ())),
                                     preferred_element_type=F32)
                if ki == qi:
                    st = jnp.where(keep, st, NEG_INF)
                m_new = jnp.maximum(m, jnp.max(st, axis=0, keepdims=True))
                alpha = jnp.exp(m - m_new)
                p = jnp.exp(st - m_new)
                l = alpha * l + jnp.sum(p, axis=0, keepdims=True)
                pv = lax.dot_general(v_ref[0, ksl, :], p.astype(BF16), (((0,), (0,)), ((), ())),
                                     preferred_element_type=F32)
                acc = alpha * acc + pv
                m = m_new
            heads.append(acc * (1.0 / l))
        o = heads[0] - lam * heads[1]
        o = o * lax.rsqrt(jnp.mean(o * o, axis=0, keepdims=True) + 1e-6)
        o_ref[0, qsl, :] = o.T * g_ref[...] * out_scale


def _attn_prompt(qq, kb, vb, lam, g, out_scale, tq):
    b, t, _ = kb.shape
    fix = lambda i, j: (0, 0)

    def qmap(sub):
        return lambda i, j: (i, 0, 2 * j + sub)

    kmap = lambda i, j: (i, 0, j)
    return pl.pallas_call(
        functools.partial(_attn_kernel, tq=tq, out_scale=out_scale),
        grid=(b, H_B),
        in_specs=[pl.BlockSpec((1, 1), fix),
                  pl.BlockSpec((1, t, LANES), qmap(0)),
                  pl.BlockSpec((1, t, LANES), qmap(1)),
                  pl.BlockSpec((1, t, LANES), kmap),
                  pl.BlockSpec((1, t, LANES), kmap),
                  pl.BlockSpec((1, DV_B), fix)],
        out_specs=pl.BlockSpec((1, t, LANES), kmap),
        out_shape=jax.ShapeDtypeStruct((b, t, D_B), F32),
        compiler_params=_cparams(("parallel", "parallel")),
        name="attn_prompt",
    )(lam, qq, qq, kb, vb, g)


def _decode_kernel(pt_ref, lam_ref, q_ref, kn_ref, vn_ref, *rest, n_grp, t_valid, out_scale):
    k_refs = rest[:n_grp]
    v_refs = rest[n_grp:2 * n_grp]
    g_ref, o_ref, qbd_s, m_s, l_s, acc_s = rest[2 * n_grp:]
    p_idx = pl.program_id(1)
    n_sub = 2 * H_B
    nr = n_sub * ROW_GROUP
    page = k_refs[0].shape[1]
    hrow = lax.broadcasted_iota(jnp.int32, (nr, 1), 0) // ROW_GROUP
    hcol = lax.broadcasted_iota(jnp.int32, (1, D_QK), 1) // DK_B

    @pl.when(p_idx == 0)
    def _():
        qt = jnp.concatenate([q_ref[...]] * n_sub, axis=0)
        qbd_s[...] = jnp.where(hrow == hcol, qt, 0.0).astype(BF16)
        m_s[...] = jnp.full(m_s.shape, NEG_INF, F32)
        l_s[...] = jnp.zeros(l_s.shape, F32)
        acc_s[...] = jnp.zeros(acc_s.shape, F32)

    def update(s, pv):
        m_prev = m_s[...]
        m_new = jnp.maximum(m_prev, jnp.max(s, axis=-1, keepdims=True))
        alpha = jnp.exp(m_prev - m_new)
        p = jnp.exp(s - m_new)
        l_s[...] = alpha * l_s[...] + jnp.sum(p, axis=-1, keepdims=True)
        acc_s[...] = alpha * acc_s[...] + pv(p.astype(BF16))
        m_s[...] = m_new

    qbd = qbd_s[...]
    s_pages = [jnp.dot(qbd, k_refs[i][...].astype(BF16), preferred_element_type=F32)
               for i in range(n_grp)]

    def pv_pages(p):
        tot = None
        for i in range(n_grp):
            vcat = jnp.concatenate(
                [v_refs[i][pl.ds(h, page, stride=H_B), :] for h in range(H_B)], axis=1)
            t = jnp.dot(p[:, i * page:(i + 1) * page], vcat.astype(BF16),
                        preferred_element_type=F32)
            tot = t if tot is None else tot + t
        return tot

    update(jnp.concatenate(s_pages, axis=1) if n_grp > 1 else s_pages[0], pv_pages)

    @pl.when(p_idx == pl.num_programs(1) - 1)
    def _():
        zpad = jnp.zeros((page - ROW_GROUP, D_QK), F32)
        kn = jnp.concatenate([kn_ref[...], zpad], axis=0).astype(BF16)
        vn = jnp.concatenate([vn_ref[...], zpad], axis=0).astype(BF16)
        r = lax.broadcasted_iota(jnp.int32, (nr, page), 0) % ROW_GROUP
        c = lax.broadcasted_iota(jnp.int32, (nr, page), 1)
        s_new = lax.dot_general(qbd, kn, (((1,), (1,)), ((), ())), preferred_element_type=F32)
        s_new = jnp.where((c <= r) & (c < t_valid), s_new, NEG_INF)
        update(s_new, lambda p: jnp.dot(p, vn, preferred_element_type=F32))
        lam = lam_ref[...]
        coef = jnp.where(hrow % 2 == 0, 1.0, -lam)
        vcol = lax.broadcasted_iota(jnp.int32, (1, D_B), 1) // DV_B
        contrib = jnp.where(vcol == hrow // 2, acc_s[...] / l_s[...] * coef, 0.0)
        o = contrib[0:ROW_GROUP]
        for h in range(1, n_sub):
            o = o + contrib[h * ROW_GROUP:(h + 1) * ROW_GROUP]
        for j in range(H_B):
            oj = o[:, j * DV_B:(j + 1) * DV_B]
            o_ref[:, j * DV_B:(j + 1) * DV_B] = _rms(oj, g_ref[...]) * out_scale


def _attn_decode(q, k_new, v_new, cache_kt, cache_v2, page_table, layer, lam, g, out_scale,
                 t_valid, n_grp):
    n = q.shape[0]
    nb, n_pages = page_table.shape
    page = cache_kt.shape[3]
    row = lambda i, p, pt: (i, 0)
    fix = lambda i, p, pt: (0, 0)

    def cmap(gi):
        return lambda i, p, pt: (layer, pt[i, p * n_grp + gi], 0, 0)

    nr = 2 * H_B * ROW_GROUP
    grid_spec = pltpu.PrefetchScalarGridSpec(
        num_scalar_prefetch=1,
        grid=(nb, n_pages // n_grp),
        in_specs=([pl.BlockSpec((1, 1), fix),
                   pl.BlockSpec((ROW_GROUP, D_QK), row),
                   pl.BlockSpec((ROW_GROUP, D_QK), row),
                   pl.BlockSpec((ROW_GROUP, D_B), row)]
                  + [pl.BlockSpec((None, None, D_QK, page), cmap(gi)) for gi in range(n_grp)]
                  + [pl.BlockSpec((None, None, page * H_B, DV_B), cmap(gi)) for gi in range(n_grp)]
                  + [pl.BlockSpec((1, DV_B), fix)]),
        out_specs=pl.BlockSpec((ROW_GROUP, D_B), row),
        scratch_shapes=[pltpu.VMEM((nr, D_QK), BF16), pltpu.VMEM((nr, 1), F32),
                        pltpu.VMEM((nr, 1), F32), pltpu.VMEM((nr, D_B), F32)],
    )
    return pl.pallas_call(
        functools.partial(_decode_kernel, n_grp=n_grp, t_valid=t_valid, out_scale=out_scale),
        grid_spec=grid_spec,
        out_shape=jax.ShapeDtypeStruct((n, D_B), F32),
        compiler_params=_cparams(("parallel", "arbitrary")),
        name="attn_decode",
    )(page_table, lam, q, k_new, v_new, *([cache_kt] * n_grp), *([cache_v2] * n_grp), g)


def _gmlp_kernel(uv_ref, w_ref, bt_ref, lg_ref, lb_ref, oc_ref, vn_ref, *, n_chunks):
    tr = lax.broadcasted_iota(jnp.int32, (CHUNK_C, CHUNK_C), 0)
    tc = lax.broadcasted_iota(jnp.int32, (CHUNK_C, CHUNK_C), 1)
    causal = tc <= tr
    grp = lax.broadcasted_iota(jnp.int32, (1, D_C), 1) // CG_C
    wgs = [jnp.where(causal, w_ref[gi], 0.0).astype(BF16) for gi in range(G_C)]
    for c in range(n_chunks):
        sl = pl.ds(c * CHUNK_C, CHUNK_C)
        u = uv_ref[sl, 0:D_C]
        v = uv_ref[sl, D_C:2 * D_C]
        mu = jnp.mean(v, axis=-1, keepdims=True)
        vc = v - mu
        var = jnp.mean(vc * vc, axis=-1, keepdims=True)
        vn = vc * lax.rsqrt(var + 1e-5) * lg_ref[...] + lb_ref[...]
        vn_ref[sl, :] = vn
        vnb = vn.astype(BF16)
        mixed = bt_ref[...]
        for gi in range(G_C):
            mg = jnp.dot(wgs[gi], vnb, preferred_element_type=F32)
            mixed = mixed + jnp.where(grp == gi, mg, 0.0)
        oc_ref[sl, :] = u * mixed


def _gmlp(uv, w, bt, lg, lb):
    n = uv.shape[0]
    n_chunks = _pick(n // CHUNK_C, (4, 2, 1))
    rows = n_chunks * CHUNK_C
    row = lambda i: (i, 0)
    fix = lambda i: (0, 0)
    return pl.pallas_call(
        functools.partial(_gmlp_kernel, n_chunks=n_chunks),
        grid=(n // rows,),
        in_specs=[pl.BlockSpec((rows, 2 * D_C), row),
                  pl.BlockSpec((G_C, CHUNK_C, CHUNK_C), lambda i: (0, 0, 0)),
                  pl.BlockSpec((CHUNK_C, D_C), fix),
                  pl.BlockSpec((1, D_C), fix), pl.BlockSpec((1, D_C), fix)],
        out_specs=[pl.BlockSpec((rows, D_C), row), pl.BlockSpec((rows, D_C), row)],
        out_shape=[jax.ShapeDtypeStruct((n, D_C), F32), jax.ShapeDtypeStruct((n, D_C), F32)],
        compiler_params=_cparams(("parallel",)),
        name="gmlp",
    )(uv, w, bt, lg, lb)


def _outproj_kernel(oa_ref, ob_ref, oc_ref, w_ref, x_ref, g_ref, o_ref):
    mix = jnp.dot(oa_ref[...].astype(BF16), w_ref[0:D_A, :], preferred_element_type=F32)
    mix += jnp.dot(ob_ref[...].astype(BF16), w_ref[D_A:D_A + D_B, :],
                   preferred_element_type=F32)
    mix += jnp.dot(oc_ref[...].astype(BF16), w_ref[D_A + D_B:, :],
                   preferred_element_type=F32)
    o_ref[...] = x_ref[...] + _rms(mix, g_ref[...])


def _outproj(oa, ob, oc, w_out, layer, x, g, tm):
    n, d = x.shape
    row = lambda i: (i, 0)
    fix = lambda i: (0, 0)
    return pl.pallas_call(
        _outproj_kernel,
        grid=(n // tm,),
        in_specs=[pl.BlockSpec((tm, D_A), row), pl.BlockSpec((tm, D_B), row),
                  pl.BlockSpec((tm, D_C), row),
                  pl.BlockSpec((None,) + w_out.shape[1:], lambda i: (layer, 0, 0)),
                  pl.BlockSpec((tm, d), row), pl.BlockSpec((1, d), fix)],
        out_specs=pl.BlockSpec((tm, d), row),
        out_shape=jax.ShapeDtypeStruct((n, d), F32),
        compiler_params=_cparams(("parallel",)),
        name="outproj",
    )(oa, ob, oc, w_out, x, g)


def _gelu_tanh(x):
    c = math.sqrt(2.0 / math.pi)
    return (0.5 * x) * (1.0 + jnp.tanh(x * (c + (c * 0.044715) * (x * x))))


def _ffn_kernel(*refs, rows, sub_rows, has_prev):
    if has_prev:
        (x_ref, gpre_ref, wg_ref, wv_ref, cwg_ref, cwv_ref, cbg_ref, cbv_ref, wd_ref, gpost_ref,
         pg_ref, pv_ref, y_ref, tg_ref, tv_ref, hn_s, acc_s) = refs
    else:
        (x_ref, gpre_ref, wg_ref, wv_ref, cwg_ref, cwv_ref, cbg_ref, cbv_ref, wd_ref, gpost_ref,
         y_ref, tg_ref, tv_ref, hn_s, acc_s) = refs
    j = pl.program_id(1)

    @pl.when(j == 0)
    def _():
        hn_s[...] = _rms(x_ref[...], gpre_ref[...]).astype(BF16)
        acc_s[...] = jnp.zeros(acc_s.shape, F32)

    tn = wg_ref.shape[1]
    n_sub = rows // sub_rows
    g8 = ROW_GROUP

    def branch(hn, w_ref, cw_ref, cb_ref, p_ref, t_ref, tail, last):
        up = jnp.dot(hn, w_ref[...], preferred_element_type=F32)
        if has_prev:
            ridx = lax.broadcasted_iota(jnp.int32, (rows, 1), 0)
            up = jnp.where(ridx % g8 >= g8 - (CONV_W - 1), p_ref[...], up)
            t_ref[...] = up
            sh1 = pltpu.roll(up, 1, 0)
            sh2 = pltpu.roll(up, 2, 0)
        else:
            if last:
                t_ref[...] = up[sub_rows - g8:, :]
            head = jnp.concatenate([tail, up[0:g8]], axis=0)
            sh1 = jnp.concatenate([pltpu.roll(head, 1, 0)[g8:], pltpu.roll(up, 1, 0)[g8:]], axis=0)
            sh2 = jnp.concatenate([pltpu.roll(head, 2, 0)[g8:], pltpu.roll(up, 2, 0)[g8:]], axis=0)
        y = cb_ref[...] + cw_ref[0:1, :] * sh2 + cw_ref[1:2, :] * sh1 + cw_ref[2:3, :] * up
        return y, up[sub_rows - g8:, :]

    tail_g = tail_v = jnp.zeros((g8, tn), F32)
    for r in range(n_sub):
        sl = pl.ds(r * sub_rows, sub_rows)
        hn = hn_s[sl, :]
        last = r == n_sub - 1
        yg, tail_g = branch(hn, wg_ref, cwg_ref, cbg_ref, pg_ref if has_prev else None, tg_ref,
                            tail_g, last)
        yv, tail_v = branch(hn, wv_ref, cwv_ref, cbv_ref, pv_ref if has_prev else None, tv_ref,
                            tail_v, last)
        hcur = (_gelu_tanh(yg) * yv).astype(BF16)
        acc_s[sl, :] += jnp.dot(hcur, wd_ref[...], preferred_element_type=F32)

    @pl.when(j == pl.num_programs(1) - 1)
    def _():
        y_ref[...] = x_ref[...] + _rms(acc_s[...], gpost_ref[...])


def _ffn(x, g_pre, w_up, conv_w, conv_b, w_down, layer, g_post, prev, rows, sub_rows, tn):
    n, d = x.shape
    d_ff = w_down.shape[1]
    nff = d_ff // tn
    has_prev = prev is not None
    assert rows % sub_rows == 0 and (not has_prev or sub_rows == rows)
    xmap = lambda i, j: (i, 0)
    fix = lambda i, j: (0, 0)
    gate = lambda i, j: (0, j)
    val = lambda i, j: (0, j + nff)
    in_specs = [pl.BlockSpec((rows, d), xmap), pl.BlockSpec((1, d), fix),
                pl.BlockSpec((None, d, tn), lambda i, j: (layer, 0, j)),
                pl.BlockSpec((None, d, tn), lambda i, j: (layer, 0, j + nff)),
                pl.BlockSpec((CONV_W, tn), gate), pl.BlockSpec((CONV_W, tn), val),
                pl.BlockSpec((1, tn), gate), pl.BlockSpec((1, tn), val),
                pl.BlockSpec((None, tn, d), lambda i, j: (layer, j, 0)), pl.BlockSpec((1, d), fix)]
    args = [x, g_pre, w_up, w_up, conv_w, conv_w, conv_b, conv_b, w_down, g_post]
    if has_prev:
        in_specs += [pl.BlockSpec((rows, tn), lambda i, j: (i, j)),
                     pl.BlockSpec((rows, tn), lambda i, j: (i, j + nff))]
        args += [prev, prev]
        t_rows = rows
    else:
        t_rows = ROW_GROUP
    t_spec = pl.BlockSpec((t_rows, tn), lambda i, j: (i, j))
    nb = n // rows
    return pl.pallas_call(
        functools.partial(_ffn_kernel, rows=rows, sub_rows=sub_rows, has_prev=has_prev),
        grid=(nb, nff),
        in_specs=in_specs,
        out_specs=[pl.BlockSpec((rows, d), xmap), t_spec, t_spec],
        out_shape=[jax.ShapeDtypeStruct((n, d), F32),
                   jax.ShapeDtypeStruct((nb * t_rows, d_ff), F32),
                   jax.ShapeDtypeStruct((nb * t_rows, d_ff), F32)],
        scratch_shapes=[pltpu.VMEM((rows, d), BF16), pltpu.VMEM((rows, d), F32)],
        compiler_params=_cparams(("parallel", "arbitrary")),
        name="ffn",
    )(*args)


def _pick(n, prefs):
    for p in prefs:
        if n % p == 0:
            return p
    return n


def kernel(x_prompt, x_sample, cache_k, cache_v, page_table, state_hgrn, state_conv, norm_mix_pre, norm_mix_post, norm_ffn_pre, norm_ffn_post, w_in, hgrn_lb, hgrn_norm, lam_q1, lam_k1, lam_q2, lam_k2, diff_norm, gmlp_ln_g, gmlp_ln_b, gmlp_ws, gmlp_bs, w_out, w_up, conv_w, conv_b, w_down):
    bp, tp, d = x_prompt.shape
    bs, ts, _ = x_sample.shape
    depth = w_in.shape[0]
    d_ff = w_down.shape[1]
    n_pool, page = cache_k.shape[1], cache_k.shape[2]
    past = page_table.shape[1] * page
    assert ts <= ROW_GROUP - (CONV_W - 1) and ts <= CHUNK_C
    assert tp % CHUNK_C == 0 and (bs * ROW_GROUP) % CHUNK_C == 0

    lb_soft = jax.nn.softmax(hgrn_lb.astype(F32), axis=0)
    lb_all = jnp.clip(jnp.cumsum(lb_soft, axis=0) - lb_soft[0], 0.0, 1.0 - 1e-6)

    tabs_p = _rope_tables(jnp.arange(tp))
    pos_s = jnp.minimum(jnp.arange(ROW_GROUP), ts - 1) + past
    tabs_s = tuple(jnp.tile(a, (bs, 1)) for a in _rope_tables(pos_s))

    ck = jnp.transpose(cache_k, (0, 1, 3, 4, 2)).reshape(depth, n_pool, D_QK, page)
    cv = cache_v.reshape(depth, n_pool, page * H_B, DV_B)
    n_grp = _pick(page_table.shape[1], (16, 8, 4, 2))

    ns = bs * ROW_GROUP
    xp = x_prompt.reshape(bp * tp, d)
    xs = jnp.pad(x_sample, ((0, 0), (0, ROW_GROUP - ts), (0, 0))).reshape(ns, d)

    tm_p = _pick(tp, (512, 256, 128))
    tq = _pick(tp, (512, 256, 128))
    tn = _pick(d_ff, (256, 128))
    tn_s = _pick(d_ff, (1408, 256, 128))
    lc_p = _pick(tp, (CHUNK_A,))
    bs_p = _pick(lc_p, (16, ROW_GROUP))
    tb_p = _pick(tp, (4 * lc_p, lc_p))
    sub_p = _pick(tp, (512, 256, 128))

    eye_g = jnp.eye(CHUNK_C // ROW_GROUP, dtype=F32)

    w_in_b, w_out_b, w_up_b, w_down_b = (w.astype(BF16) for w in (w_in, w_out, w_up, w_down))

    outs = {k: [] for k in ("k_p", "v_p", "k_s", "v_s", "hg_p", "hg_s", "gv_s", "cv_p", "cv_s")}
    for l in range(depth):
        lam_init = 0.8 - 0.6 * math.exp(-0.3 * l)
        lam = (jnp.exp(jnp.sum(lam_q1[l].astype(F32) * lam_k1[l].astype(F32)))
               - jnp.exp(jnp.sum(lam_q2[l].astype(F32) * lam_k2[l].astype(F32)))
               + lam_init).reshape(1, 1)
        lb = lb_all[l]
        lbp = jnp.stack([jnp.log(lb), jnp.log1p(-lb), 1.0 - lb])
        g_hgrn = jnp.tile(hgrn_norm[l], H_A)[None, :]
        g_diff = diff_norm[l][None, :]
        ws = gmlp_ws[l].astype(F32)
        bsl = gmlp_bs[l].astype(F32)
        bt_p = jnp.repeat(bsl.T, CG_C, axis=1)
        ws8 = jnp.pad(ws[:, :ts, :ts], ((0, 0), (0, ROW_GROUP - ts), (0, ROW_GROUP - ts)))
        ws_s = jnp.einsum('ab,gts->gatbs', eye_g, ws8).reshape(G_C, CHUNK_C, CHUNK_C)
        bt_s = jnp.tile(jnp.repeat(bsl.T[:ROW_GROUP], CG_C, axis=1), (CHUNK_C // ROW_GROUP, 1))
        lg = gmlp_ln_g[l][None, :]
        lbn = gmlp_ln_b[l][None, :]
        cb = conv_b[l][None, :]

        hg, qq, k, v, kb, vb, uv = _proj(xp, norm_mix_pre[l][None, :], w_in_b, l, tabs_p, tm_p, True)
        oa, st = _hgrn(hg.reshape(bp, tp, -1), lbp, g_hgrn, None, lc_p, bs_p, tb_p, 1, lc_p)
        seq = lambda a: a.reshape(bp, tp, -1)
        ob = _attn_prompt(seq(qq), seq(kb), seq(vb), lam, g_diff, 1.0 - lam_init, tq)
        oc, _ = _gmlp(uv, ws, bt_p, lg, lbn)
        x1 = _outproj(oa.reshape(-1, D_A), ob.reshape(-1, D_B), oc, w_out_b, l, xp,
                      norm_mix_post[l][None, :], tm_p)
        xp, tg, tv = _ffn(x1, norm_ffn_pre[l][None, :], w_up_b, conv_w[l], cb, w_down_b, l,
                          norm_ffn_post[l][None, :], None, tp, sub_p, tn)
        outs["k_p"].append(jnp.transpose(k.reshape(bp, 2 * H_B, DK_B, tp), (0, 3, 1, 2)))
        outs["v_p"].append(v.reshape(bp, tp, H_B, DV_B))
        outs["hg_p"].append(st)
        tail = jnp.concatenate([tg, tv], axis=1).reshape(bp, ROW_GROUP, 2 * d_ff)
        outs["cv_p"].append(tail[:, ROW_GROUP - (CONV_W - 1):])

        hg, q, k, v, uv = _proj(xs, norm_mix_pre[l][None, :], w_in_b, l, tabs_s, ns, False)
        oa, st = _hgrn(hg.reshape(bs, ROW_GROUP, -1), lbp, g_hgrn, state_hgrn[l],
                       ROW_GROUP, ROW_GROUP, ROW_GROUP, _pick(bs, (8, 4, 2, 1)), ts)
        ob = _attn_decode(q, k, v, ck, cv, page_table, l, lam, g_diff, 1.0 - lam_init, ts, n_grp)
        oc, vn = _gmlp(uv, ws_s, bt_s, lg, lbn)
        x1 = _outproj(oa.reshape(-1, D_A), ob, oc, w_out_b, l, xs, norm_mix_post[l][None, :], ns)
        prev = jnp.pad(jnp.roll(state_conv[l].astype(F32), -1, axis=0),
                       ((0, 0), (ROW_GROUP - (CONV_W - 1), 0), (0, 0))).reshape(ns, 2 * d_ff)
        xs, tg, tv = _ffn(x1, norm_ffn_pre[l][None, :], w_up_b, conv_w[l], cb, w_down_b, l,
                          norm_ffn_post[l][None, :], prev, ns, ns, tn_s)
        grp = lambda a: a.reshape(bs, ROW_GROUP, -1)[:, :ts]
        outs["k_s"].append(grp(k).reshape(bs, ts, 2 * H_B, DK_B))
        outs["v_s"].append(grp(v).reshape(bs, ts, H_B, DV_B))
        outs["hg_s"].append(st)
        outs["gv_s"].append(grp(vn))
        up_s = jnp.concatenate([tg, tv], axis=1).reshape(bs, ROW_GROUP, 2 * d_ff)
        outs["cv_s"].append(up_s[:, ts - (CONV_W - 1):ts])

    y_p = xp.reshape(bp, tp, d)
    y_s = xs.reshape(bs, ROW_GROUP, d)[:, :ts]
    st = lambda key: jnp.stack(outs[key])
    return (y_p, y_s, st("k_p"), st("v_p"), st("k_s"), st("v_s"), st("hg_p"), st("hg_s"),
            st("gv_s"), st("cv_p"), st("cv_s"))
```

```python
import functools
import math

import jax
import jax.numpy as jnp
from jax import lax
from jax.experimental import pallas as pl
from jax.experimental.pallas import tpu as pltpu

F32 = jnp.float32
BF16 = jnp.bfloat16

H_A = 4
DK_A = 64
D_A = H_A * DK_A
H_B = 4
DK_B = 64
DV_B = 128
D_QK = 2 * H_B * DK_B
D_B = H_B * DV_B
ROT_DIM = DK_B // 4
ROPE_THETA = 500000.0
G_C = 4
D_C = 256
CG_C = D_C // G_C
CHUNK_C = 128
CHUNK_A = 64
CONV_W = 3
ROW_GROUP = 8
LANES = 128
VMEM_LIMIT = 56 * 1024 * 1024
NEG_INF = float("-inf")


def _cparams(sem):
    return pltpu.CompilerParams(dimension_semantics=sem, vmem_limit_bytes=VMEM_LIMIT)


def _rms(x, g, eps=1e-6):
    return x * lax.rsqrt(jnp.mean(x * x, axis=-1, keepdims=True) + eps) * g


def _proj_kernel(x_ref, g_ref, w_ref, cos_ref, sa_ref, sb_ref, *refs, q_scale, attn_bf16,
                 nt, lc, bs):
    if attn_bf16:
        (lb_ref, gh_ref, oa_ref, q_ref, k_ref, v_ref, kb_ref, vb_ref, uv_ref, st_ref,
         st_s, cum_s, kk_s, v_s) = refs
    else:
        hg_ref, q_ref, k_ref, v_ref, uv_ref = refs
    h = _rms(x_ref[...], g_ref[...]).astype(BF16)

    def mm(lo, hi):
        return jnp.dot(h, w_ref[:, lo:hi], preferred_element_type=F32)

    n_hg = 4 * D_A
    if attn_bf16:
        step = pl.program_id(0)

        @pl.when(step % nt == 0)
        def _():
            st_s[...] = jnp.zeros(st_s.shape, F32)

        hg = mm(0, n_hg)
        st = st_s[...]
        for c in range(hg.shape[0] // lc):
            part = lambda k: hg[c * lc:(c + 1) * lc, k * D_A:(k + 1) * D_A]
            oa, st = _hgrn_chunk(part(0), part(1), part(2), part(3), st, lb_ref, gh_ref,
                                 cum_s.at[c], kk_s.at[c], v_s.at[c], lc=lc, bs=bs, t_valid=lc)
            oa_ref[c * lc:(c + 1) * lc, :] = oa
        st_s[...] = st

        @pl.when(step % nt == nt - 1)
        def _():
            st_ref[0] = _state_rows(st)
    else:
        hg_ref[...] = mm(0, n_hg)
    cos = cos_ref[...]
    sa = sa_ref[...]
    sb = sb_ref[...]
    half = ROT_DIM // 2

    def rope(xx):
        return (xx * cos + pltpu.roll(xx, LANES - half, 1) * sa
                + pltpu.roll(xx, half, 1) * sb)

    first = lax.broadcasted_iota(jnp.int32, (1, LANES), 1) < DK_B
    for j in range(D_QK // LANES):
        lo = n_hg + j * LANES
        qj = rope(mm(lo, lo + LANES)) * q_scale
        lo = n_hg + D_QK + j * LANES
        kj = rope(mm(lo, lo + LANES))
        if attn_bf16:
            k_ref[j * LANES:(j + 1) * LANES, :] = kj.T
            q_ref[:, 2 * j * LANES:(2 * j + 1) * LANES] = jnp.where(first, qj, 0.0).astype(BF16)
            q_ref[:, (2 * j + 1) * LANES:(2 * j + 2) * LANES] = jnp.where(first, 0.0, qj).astype(BF16)
            kb_ref[:, j * LANES:(j + 1) * LANES] = kj.astype(BF16)
        else:
            k_ref[:, j * LANES:(j + 1) * LANES] = kj
            q_ref[:, j * LANES:(j + 1) * LANES] = qj
    lo = n_hg + 2 * D_QK
    vv = mm(lo, lo + D_B)
    if attn_bf16:
        for hh in range(H_B):
            v_ref[:, hh, :] = vv[:, hh * DV_B:(hh + 1) * DV_B]
        vb_ref[...] = vv.astype(BF16)
    else:
        v_ref[...] = vv
    uv_ref[...] = mm(lo + D_B, lo + D_B + 2 * D_C)


def _proj(x, g, w_in, layer, tabs, tm, hgrn=None):
    n, d = x.shape
    n_in = w_in.shape[2]
    cos, sa, sb = tabs
    t_len = cos.shape[0]
    nt = t_len // tm
    row = lambda i: (i, 0)
    fix = lambda i: (0, 0)
    tab = lambda i: (i % nt, 0)
    fused = hgrn is not None
    in_specs = [pl.BlockSpec((tm, d), row), pl.BlockSpec((1, d), fix),
                pl.BlockSpec((None, d, n_in), lambda i: (layer, 0, 0)),
                pl.BlockSpec((tm, LANES), tab), pl.BlockSpec((tm, LANES), tab),
                pl.BlockSpec((tm, LANES), tab)]
    args = [x, g, w_in, cos, sa, sb]
    scratch = []
    lc = bs = 0
    if fused:
        lbp, g_hgrn, lc, bs = hgrn
        in_specs += [pl.BlockSpec((3, D_A), fix), pl.BlockSpec((1, D_A), fix)]
        args += [lbp, g_hgrn]
        outs = [(D_A, F32), (2 * D_QK, BF16), None, None, (D_QK, BF16), (D_B, BF16),
                (2 * D_C, F32), None]
    else:
        outs = [(4 * D_A, F32), (D_QK, F32), (D_QK, F32), (D_B, F32), (2 * D_C, F32)]
    out_specs = [o and pl.BlockSpec((tm, o[0]), row) for o in outs]
    out_shape = [o and jax.ShapeDtypeStruct((n, o[0]), o[1]) for o in outs]
    if fused:
        nb = n // t_len
        out_specs[2] = pl.BlockSpec((None, D_QK, tm), lambda i: (i // nt, 0, i % nt))
        out_shape[2] = jax.ShapeDtypeStruct((nb, D_QK, t_len), F32)
        out_specs[3] = pl.BlockSpec((tm, H_B, DV_B), lambda i: (i, 0, 0))
        out_shape[3] = jax.ShapeDtypeStruct((n, H_B, DV_B), F32)
        out_specs[7] = pl.BlockSpec((1, D_A, DK_A), lambda i: (i // nt, 0, 0))
        out_shape[7] = jax.ShapeDtypeStruct((nb, D_A, DK_A), F32)
        scratch = [pltpu.VMEM((D_A, D_A), F32)] + [pltpu.VMEM((tm // lc, lc, D_A), F32)] * 3
    res = pl.pallas_call(
        functools.partial(_proj_kernel, q_scale=DK_B ** -0.5, attn_bf16=fused, nt=nt, lc=lc, bs=bs),
        grid=(n // tm,),
        in_specs=in_specs,
        out_specs=out_specs,
        out_shape=out_shape,
        scratch_shapes=scratch,
        compiler_params=_cparams(("arbitrary",)),
        name="proj",
    )(*args)
    if fused:
        res = list(res)
        res[7] = res[7].reshape(n // t_len, H_A, DK_A, DK_A)
    return res


def _rope_tables(pos):
    half = ROT_DIM // 2
    inv = ROPE_THETA ** (-jnp.arange(half, dtype=F32) * (2.0 / ROT_DIM))
    ang = pos.astype(F32)[:, None] * inv[None, :]
    c, s = jnp.cos(ang), jnp.sin(ang)
    t = pos.shape[0]
    rest = DK_B - ROT_DIM
    cos64 = jnp.concatenate([c, c, jnp.ones((t, rest), F32)], axis=1)
    sa64 = jnp.concatenate([-s, jnp.zeros((t, half + rest), F32)], axis=1)
    sb64 = jnp.concatenate([jnp.zeros((t, half), F32), s, jnp.zeros((t, rest), F32)], axis=1)
    rep = LANES // DK_B
    return tuple(jnp.tile(a, (1, rep)) for a in (cos64, sa64, sb64))


def _hgrn_chunk(q, z, vi, ga, st, lb_ref, g_ref, cum_s, kk_s, v_s, *, lc, bs, t_valid):
    nblk = lc // bs
    log_sig = jnp.minimum(z, 0.0) - jnp.log1p(jnp.exp(-jnp.abs(z)))
    a = lb_ref[0:1, :]
    b = lb_ref[1:2, :] + log_sig
    log_f = jnp.maximum(a, b) + jnp.log1p(jnp.exp(-jnp.abs(a - b)))
    kk = lb_ref[2:3, :] * jax.nn.sigmoid(-z)
    rows = lax.broadcasted_iota(jnp.int32, (lc, 1), 0)
    if t_valid < lc:
        log_f = jnp.where(rows < t_valid, log_f, 0.0)
        kk = jnp.where(rows < t_valid, kk, 0.0)

    cum = log_f
    d = 1
    while d < lc:
        cum = cum + jnp.where(rows >= d, pltpu.roll(cum, d, 0), 0.0)
        d *= 2

    cum_s[...] = cum
    kk_s[...] = kk
    v_s[...] = vi

    hr = lax.broadcasted_iota(jnp.int32, (D_A, D_A), 0) // DK_A
    hc = lax.broadcasted_iota(jnp.int32, (D_A, D_A), 1) // DK_A
    same_head = hr == hc
    ones_bd = same_head.astype(BF16)

    q_dec = (q * jnp.exp(cum)).astype(BF16)
    o_inter = lax.dot_general(q_dec, st.astype(BF16), (((1,), (1,)), ((), ())),
                              preferred_element_type=F32)
    o_blk = [o_inter[i * bs:(i + 1) * bs] for i in range(nblk)]

    sub = ROW_GROUP
    for i in range(nblk):
        r0 = i * bs
        pieces = []
        for s in range(bs):
            lo = r0 + (s // sub) * sub
            cs = cum_s[r0 + s:r0 + s + 1, :]
            ks = kk_s[r0 + s:r0 + s + 1, :]
            dd = cum[lo:r0 + bs] - cs
            if s % sub:
                dd = jnp.where(rows[lo:r0 + bs] >= r0 + s, dd, NEG_INF)
            pieces.append(q[lo:r0 + bs] * jnp.exp(dd) * ks)
        p_all = jnp.dot(jnp.concatenate(pieces, axis=0).astype(BF16), ones_bd,
                        preferred_element_type=F32)
        off = 0
        acc = {}
        for s in range(bs):
            lo = (s // sub) * sub
            n = bs - lo
            contrib = p_all[off:off + n] * v_s[r0 + s:r0 + s + 1, :]
            acc[lo] = contrib if lo not in acc else acc[lo] + contrib
            off += n
        tot = acc[0]
        for lo, val in acc.items():
            if lo:
                tot = tot + jnp.concatenate([jnp.zeros((lo, D_A), F32), val], axis=0)
        o_blk[i] = o_blk[i] + tot

    if nblk > 1:
        hm = (lax.broadcasted_iota(jnp.int32, (H_A * bs, D_A), 0) // bs
              == lax.broadcasted_iota(jnp.int32, (H_A * bs, D_A), 1) // DK_A)
        for j in range(nblk - 1):
            r1 = (j + 1) * bs
            aj = cum[r1 - 1:r1, :]
            k_t = kk[r1 - bs:r1] * jnp.exp(aj - cum[r1 - bs:r1])
            q_t = (q[r1:] * jnp.exp(cum[r1:] - aj)).astype(BF16)
            k_bd = jnp.where(hm, jnp.concatenate([k_t] * H_A, axis=0), 0.0).astype(BF16)
            v_bd = jnp.where(hm, jnp.concatenate([vi[r1 - bs:r1]] * H_A, axis=0), 0.0).astype(BF16)
            s2 = lax.dot_general(q_t, k_bd, (((1,), (1,)), ((), ())),
                                 preferred_element_type=F32)
            o_off = jnp.dot(s2.astype(BF16), v_bd, preferred_element_type=F32)
            for i in range(j + 1, nblk):
                o_blk[i] = o_blk[i] + o_off[(i - j - 1) * bs:(i - j) * bs]

    last = cum[lc - 1:lc, :]
    k_dec = (kk * jnp.exp(last - cum)).astype(BF16)
    upd = lax.dot_general(vi.astype(BF16), k_dec, (((0,), (0,)), ((), ())),
                          preferred_element_type=F32)
    st_new = st * jnp.exp(last) + jnp.where(same_head, upd, 0.0)

    o = jnp.concatenate(o_blk, axis=0) if nblk > 1 else o_blk[0]
    sq = o * o
    hi = sq.astype(BF16)
    lo_ = (sq - hi.astype(F32)).astype(BF16)
    ms = (jnp.dot(hi, ones_bd, preferred_element_type=F32)
          + jnp.dot(lo_, ones_bd, preferred_element_type=F32)) * (1.0 / DK_A)
    oa = o * lax.rsqrt(ms + 1e-6) * g_ref[...] * (ga * jax.nn.sigmoid(ga))
    return oa, st_new


def _state_rows(st):
    bd = st.T
    tot = bd[:, 0:DK_A]
    for h in range(1, H_A):
        tot = tot + bd[:, h * DK_A:(h + 1) * DK_A]
    return tot


def _hgrn_kernel(*refs, lc, bs, n_chunks, n_seq, t_valid, zero_init):
    if zero_init:
        hg_ref, lb_ref, g_ref, oa_ref, st_ref, st_s, cum_s, kk_s, v_s = refs
    else:
        hg_ref, lb_ref, g_ref, st0_ref, oa_ref, st_ref, st_s, cum_s, kk_s, v_s = refs
    t_idx = pl.program_id(1)

    @pl.when(t_idx == 0)
    def _():
        if zero_init:
            st_s[...] = jnp.zeros(st_s.shape, F32)
        else:
            same_head = (lax.broadcasted_iota(jnp.int32, (D_A, D_A), 0) // DK_A
                         == lax.broadcasted_iota(jnp.int32, (D_A, D_A), 1) // DK_A)
            for i in range(n_seq):
                x4 = jnp.concatenate([st0_ref[i]] * H_A, axis=1)
                st_s[i] = jnp.where(same_head, x4, 0.0).T

    for i in range(n_seq):
        for ci in range(n_chunks):
            rows = pl.ds(ci * lc, lc)
            blk = lambda k: hg_ref[i, rows, k * D_A:(k + 1) * D_A]
            oa, st_new = _hgrn_chunk(blk(0), blk(1), blk(2), blk(3), st_s[i], lb_ref, g_ref,
                                     cum_s.at[i], kk_s.at[i], v_s.at[i],
                                     lc=lc, bs=bs, t_valid=t_valid)
            oa_ref[i, rows, :] = oa
            st_s[i] = st_new

    @pl.when(t_idx == pl.num_programs(1) - 1)
    def _():
        for i in range(n_seq):
            st_ref[i] = _state_rows(st_s[i])


def _hgrn(hg, lbp, g, st0, lc, bs, tb, n_seq, t_valid):
    b, t, w = hg.shape
    zero_init = st0 is None
    smap = lambda i, c: (i, 0, 0)
    in_specs = [pl.BlockSpec((n_seq, tb, w), lambda i, c: (i, c, 0)),
                pl.BlockSpec((3, D_A), lambda i, c: (0, 0)),
                pl.BlockSpec((1, D_A), lambda i, c: (0, 0))]
    args = [hg, lbp, g]
    if not zero_init:
        in_specs.append(pl.BlockSpec((n_seq, D_A, DK_A), smap))
        args.append(st0.astype(F32).reshape(b, D_A, DK_A))
    oa, st = pl.pallas_call(
        functools.partial(_hgrn_kernel, lc=lc, bs=bs, n_chunks=tb // lc, n_seq=n_seq,
                          t_valid=t_valid, zero_init=zero_init),
        grid=(b // n_seq, t // tb),
        in_specs=in_specs,
        out_specs=[pl.BlockSpec((n_seq, tb, D_A), lambda i, c: (i, c, 0)),
                   pl.BlockSpec((n_seq, D_A, DK_A), smap)],
        out_shape=[jax.ShapeDtypeStruct((b, t, D_A), F32),
                   jax.ShapeDtypeStruct((b, D_A, DK_A), F32)],
        scratch_shapes=([pltpu.VMEM((n_seq, D_A, D_A), F32)]
                        + [pltpu.VMEM((n_seq, lc, D_A), F32)] * 3),
        compiler_params=_cparams(("parallel", "arbitrary")),
        name="hgrn",
    )(*args)
    return oa, st.reshape(b, H_A, DK_A, DK_A)


def _attn_kernel(lam_ref, q1_ref, q2_ref, k_ref, v_ref, g_ref, o_ref, *, tq, out_scale):
    t = k_ref.shape[1]
    nq = t // tq
    keep = (lax.broadcasted_iota(jnp.int32, (tq, tq), 0)
            <= lax.broadcasted_iota(jnp.int32, (tq, tq), 1))
    lam = lam_ref[...]
    for qi in range(nq):
        qsl = pl.ds(qi * tq, tq)
        heads = []
        for q_ref in (q1_ref, q2_ref):
            q = q_ref[0, qsl, :]
            m = jnp.full((1, tq), NEG_INF, F32)
            l = jnp.zeros((1, tq), F32)
            acc = jnp.zeros((DV_B, tq), F32)
            for ki in range(qi + 1):
                ksl = pl.ds(ki * tq, tq)
                st = lax.dot_general(k_ref[0, ksl, :], q, (((1,), (1,)), ((), ())),
                                     preferred_element_type=F32)
                if ki == qi:
                    st = jnp.where(keep, st, NEG_INF)
                m_new = jnp.maximum(m, jnp.max(st, axis=0, keepdims=True))
                alpha = jnp.exp(m - m_new)
                p = jnp.exp(st - m_new)
                l = alpha * l + jnp.sum(p, axis=0, keepdims=True)
                pv = lax.dot_general(v_ref[0, ksl, :], p.astype(BF16), (((0,), (0,)), ((), ())),
                                     preferred_element_type=F32)
                acc = alpha * acc + pv
                m = m_new
            heads.append(acc * (1.0 / l))
        o = heads[0] - lam * heads[1]
        o = o * lax.rsqrt(jnp.mean(o * o, axis=0, keepdims=True) + 1e-6)
        o_ref[0, qsl, :] = o.T * g_ref[...] * out_scale


def _attn_prompt(qq, kb, vb, lam, g, out_scale, tq):
    b, t, _ = kb.shape
    fix = lambda i, j: (0, 0)

    def qmap(sub):
        return lambda i, j: (i, 0, 2 * j + sub)

    kmap = lambda i, j: (i, 0, j)
    return pl.pallas_call(
        functools.partial(_attn_kernel, tq=tq, out_scale=out_scale),
        grid=(b, H_B),
        in_specs=[pl.BlockSpec((1, 1), fix),
                  pl.BlockSpec((1, t, LANES), qmap(0)),
                  pl.BlockSpec((1, t, LANES), qmap(1)),
                  pl.BlockSpec((1, t, LANES), kmap),
                  pl.BlockSpec((1, t, LANES), kmap),
                  pl.BlockSpec((1, DV_B), fix)],
        out_specs=pl.BlockSpec((1, t, LANES), kmap),
        out_shape=jax.ShapeDtypeStruct((b, t, D_B), F32),
        compiler_params=_cparams(("parallel", "parallel")),
        name="attn_prompt",
    )(lam, qq, qq, kb, vb, g)


def _decode_kernel(pt_ref, lam_ref, q_ref, kn_ref, vn_ref, *rest, n_grp, t_valid, out_scale):
    k_refs = rest[:n_grp]
    v_refs = rest[n_grp:2 * n_grp]
    g_ref, o_ref, qbd_s, m_s, l_s, acc_s = rest[2 * n_grp:]
    p_idx = pl.program_id(1)
    n_sub = 2 * H_B
    nr = n_sub * ROW_GROUP
    page = k_refs[0].shape[1]
    hrow = lax.broadcasted_iota(jnp.int32, (nr, 1), 0) // ROW_GROUP
    hcol = lax.broadcasted_iota(jnp.int32, (1, D_QK), 1) // DK_B

    @pl.when(p_idx == 0)
    def _():
        qt = jnp.concatenate([q_ref[...]] * n_sub, axis=0)
        qbd_s[...] = jnp.where(hrow == hcol, qt, 0.0).astype(BF16)
        m_s[...] = jnp.full(m_s.shape, NEG_INF, F32)
        l_s[...] = jnp.zeros(l_s.shape, F32)
        acc_s[...] = jnp.zeros(acc_s.shape, F32)

    def update(s, pv):
        m_prev = m_s[...]
        m_new = jnp.maximum(m_prev, jnp.max(s, axis=-1, keepdims=True))
        alpha = jnp.exp(m_prev - m_new)
        p = jnp.exp(s - m_new)
        l_s[...] = alpha * l_s[...] + jnp.sum(p, axis=-1, keepdims=True)
        acc_s[...] = alpha * acc_s[...] + pv(p.astype(BF16))
        m_s[...] = m_new

    qbd = qbd_s[...]
    s_pages = [jnp.dot(qbd, k_refs[i][...].astype(BF16), preferred_element_type=F32)
               for i in range(n_grp)]

    def pv_pages(p):
        tot = None
        for i in range(n_grp):
            vcat = jnp.concatenate(
                [v_refs[i][pl.ds(h, page, stride=H_B), :] for h in range(H_B)], axis=1)
            t = jnp.dot(p[:, i * page:(i + 1) * page], vcat.astype(BF16),
                        preferred_element_type=F32)
            tot = t if tot is None else tot + t
        return tot

    update(jnp.concatenate(s_pages, axis=1) if n_grp > 1 else s_pages[0], pv_pages)

    @pl.when(p_idx == pl.num_programs(1) - 1)
    def _():
        zpad = jnp.zeros((page - ROW_GROUP, D_QK), F32)
        kn = jnp.concatenate([kn_ref[...], zpad], axis=0).astype(BF16)
        vn = jnp.concatenate([vn_ref[...], zpad], axis=0).astype(BF16)
        r = lax.broadcasted_iota(jnp.int32, (nr, page), 0) % ROW_GROUP
        c = lax.broadcasted_iota(jnp.int32, (nr, page), 1)
        s_new = lax.dot_general(qbd, kn, (((1,), (1,)), ((), ())), preferred_element_type=F32)
        s_new = jnp.where((c <= r) & (c < t_valid), s_new, NEG_INF)
        update(s_new, lambda p: jnp.dot(p, vn, preferred_element_type=F32))
        lam = lam_ref[...]
        coef = jnp.where(hrow % 2 == 0, 1.0, -lam)
        vcol = lax.broadcasted_iota(jnp.int32, (1, D_B), 1) // DV_B
        contrib = jnp.where(vcol == hrow // 2, acc_s[...] / l_s[...] * coef, 0.0)
        o = contrib[0:ROW_GROUP]
        for h in range(1, n_sub):
            o = o + contrib[h * ROW_GROUP:(h + 1) * ROW_GROUP]
        for j in range(H_B):
            oj = o[:, j * DV_B:(j + 1) * DV_B]
            o_ref[:, j * DV_B:(j + 1) * DV_B] = _rms(oj, g_ref[...]) * out_scale


def _attn_decode(q, k_new, v_new, cache_kt, cache_v2, page_table, layer, lam, g, out_scale,
                 t_valid, n_grp):
    n = q.shape[0]
    nb, n_pages = page_table.shape
    page = cache_kt.shape[3]
    row = lambda i, p, pt: (i, 0)
    fix = lambda i, p, pt: (0, 0)

    def cmap(gi):
        return lambda i, p, pt: (layer, pt[i, p * n_grp + gi], 0, 0)

    nr = 2 * H_B * ROW_GROUP
    grid_spec = pltpu.PrefetchScalarGridSpec(
        num_scalar_prefetch=1,
        grid=(nb, n_pages // n_grp),
        in_specs=([pl.BlockSpec((1, 1), fix),
                   pl.BlockSpec((ROW_GROUP, D_QK), row),
                   pl.BlockSpec((ROW_GROUP, D_QK), row),
                   pl.BlockSpec((ROW_GROUP, D_B), row)]
                  + [pl.BlockSpec((None, None, D_QK, page), cmap(gi)) for gi in range(n_grp)]
                  + [pl.BlockSpec((None, None, page * H_B, DV_B), cmap(gi)) for gi in range(n_grp)]
                  + [pl.BlockSpec((1, DV_B), fix)]),
        out_specs=pl.BlockSpec((ROW_GROUP, D_B), row),
        scratch_shapes=[pltpu.VMEM((nr, D_QK), BF16), pltpu.VMEM((nr, 1), F32),
                        pltpu.VMEM((nr, 1), F32), pltpu.VMEM((nr, D_B), F32)],
    )
    return pl.pallas_call(
        functools.partial(_decode_kernel, n_grp=n_grp, t_valid=t_valid, out_scale=out_scale),
        grid_spec=grid_spec,
        out_shape=jax.ShapeDtypeStruct((n, D_B), F32),
        compiler_params=_cparams(("parallel", "arbitrary")),
        name="attn_decode",
    )(page_table, lam, q, k_new, v_new, *([cache_kt] * n_grp), *([cache_v2] * n_grp), g)


def _outproj_kernel(oa_ref, ob_ref, uv_ref, ws_ref, bt_ref, lg_ref, lb_ref, w_ref, x_ref, g_ref,
                    o_ref, *vn_out, n_chunks):
    tr = lax.broadcasted_iota(jnp.int32, (CHUNK_C, CHUNK_C), 0)
    tc = lax.broadcasted_iota(jnp.int32, (CHUNK_C, CHUNK_C), 1)
    causal = tc <= tr
    grp = lax.broadcasted_iota(jnp.int32, (1, D_C), 1) // CG_C
    wgs = [jnp.where(causal, ws_ref[gi], 0.0).astype(BF16) for gi in range(G_C)]
    ocs = []
    for c in range(n_chunks):
        sl = pl.ds(c * CHUNK_C, CHUNK_C)
        u = uv_ref[sl, 0:D_C]
        v = uv_ref[sl, D_C:2 * D_C]
        mu = jnp.mean(v, axis=-1, keepdims=True)
        vc = v - mu
        var = jnp.mean(vc * vc, axis=-1, keepdims=True)
        vn = vc * lax.rsqrt(var + 1e-5) * lg_ref[...] + lb_ref[...]
        if vn_out:
            vn_out[0][sl, :] = vn
        vnb = vn.astype(BF16)
        mixed = bt_ref[...]
        for gi in range(G_C):
            mg = jnp.dot(wgs[gi], vnb, preferred_element_type=F32)
            mixed = mixed + jnp.where(grp == gi, mg, 0.0)
        ocs.append((u * mixed).astype(BF16))
    oc = jnp.concatenate(ocs, axis=0) if n_chunks > 1 else ocs[0]
    mix = jnp.dot(oa_ref[...].astype(BF16), w_ref[0:D_A, :], preferred_element_type=F32)
    mix += jnp.dot(ob_ref[...].astype(BF16), w_ref[D_A:D_A + D_B, :],
                   preferred_element_type=F32)
    mix += jnp.dot(oc, w_ref[D_A + D_B:, :], preferred_element_type=F32)
    o_ref[...] = x_ref[...] + _rms(mix, g_ref[...])


def _outproj(oa, ob, uv, ws, bt, lg, lb, w_out, layer, x, g, tm, want_vn):
    n, d = x.shape
    row = lambda i: (i, 0)
    fix = lambda i: (0, 0)
    out_specs = [pl.BlockSpec((tm, d), row)]
    out_shape = [jax.ShapeDtypeStruct((n, d), F32)]
    if want_vn:
        out_specs.append(pl.BlockSpec((tm, D_C), row))
        out_shape.append(jax.ShapeDtypeStruct((n, D_C), F32))
    res = pl.pallas_call(
        functools.partial(_outproj_kernel, n_chunks=tm // CHUNK_C),
        grid=(n // tm,),
        in_specs=[pl.BlockSpec((tm, D_A), row), pl.BlockSpec((tm, D_B), row),
                  pl.BlockSpec((tm, 2 * D_C), row),
                  pl.BlockSpec((G_C, CHUNK_C, CHUNK_C), lambda i: (0, 0, 0)),
                  pl.BlockSpec((CHUNK_C, D_C), fix),
                  pl.BlockSpec((1, D_C), fix), pl.BlockSpec((1, D_C), fix),
                  pl.BlockSpec((None,) + w_out.shape[1:], lambda i: (layer, 0, 0)),
                  pl.BlockSpec((tm, d), row), pl.BlockSpec((1, d), fix)],
        out_specs=out_specs,
        out_shape=out_shape,
        compiler_params=_cparams(("parallel",)),
        name="outproj",
    )(oa, ob, uv, ws, bt, lg, lb, w_out, x, g)
    return res if want_vn else (res[0], None)


def _gelu_tanh(x):
    c = math.sqrt(2.0 / math.pi)
    return (0.5 * x) * (1.0 + jnp.tanh(x * (c + (c * 0.044715) * (x * x))))


def _ffn_kernel(*refs, rows, sub_rows, has_prev):
    if has_prev:
        (x_ref, gpre_ref, wg_ref, wv_ref, cwg_ref, cwv_ref, cbg_ref, cbv_ref, wd_ref, gpost_ref,
         pg_ref, pv_ref, y_ref, tg_ref, tv_ref, hn_s, acc_s) = refs
    else:
        (x_ref, gpre_ref, wg_ref, wv_ref, cwg_ref, cwv_ref, cbg_ref, cbv_ref, wd_ref, gpost_ref,
         y_ref, tg_ref, tv_ref, hn_s, acc_s) = refs
    j = pl.program_id(1)

    @pl.when(j == 0)
    def _():
        hn_s[...] = _rms(x_ref[...], gpre_ref[...]).astype(BF16)
        acc_s[...] = jnp.zeros(acc_s.shape, F32)

    tn = wg_ref.shape[1]
    n_sub = rows // sub_rows
    g8 = ROW_GROUP

    def branch(hn, w_ref, cw_ref, cb_ref, p_ref, t_ref, tail, last):
        up = jnp.dot(hn, w_ref[...], preferred_element_type=F32)
        if has_prev:
            ridx = lax.broadcasted_iota(jnp.int32, (rows, 1), 0)
            up = jnp.where(ridx % g8 >= g8 - (CONV_W - 1), p_ref[...], up)
            t_ref[...] = up
            sh1 = pltpu.roll(up, 1, 0)
            sh2 = pltpu.roll(up, 2, 0)
        else:
            if last:
                t_ref[...] = up[sub_rows - g8:, :]
            head = jnp.concatenate([tail, up[0:g8]], axis=0)
            sh1 = jnp.concatenate([pltpu.roll(head, 1, 0)[g8:], pltpu.roll(up, 1, 0)[g8:]], axis=0)
            sh2 = jnp.concatenate([pltpu.roll(head, 2, 0)[g8:], pltpu.roll(up, 2, 0)[g8:]], axis=0)
        y = cb_ref[...] + cw_ref[0:1, :] * sh2 + cw_ref[1:2, :] * sh1 + cw_ref[2:3, :] * up
        return y, up[sub_rows - g8:, :]

    tail_g = tail_v = jnp.zeros((g8, tn), F32)
    for r in range(n_sub):
        sl = pl.ds(r * sub_rows, sub_rows)
        hn = hn_s[sl, :]
        last = r == n_sub - 1
        yg, tail_g = branch(hn, wg_ref, cwg_ref, cbg_ref, pg_ref if has_prev else None, tg_ref,
                            tail_g, last)
        yv, tail_v = branch(hn, wv_ref, cwv_ref, cbv_ref, pv_ref if has_prev else None, tv_ref,
                            tail_v, last)
        hcur = (_gelu_tanh(yg) * yv).astype(BF16)
        acc_s[sl, :] += jnp.dot(hcur, wd_ref[...], preferred_element_type=F32)

    @pl.when(j == pl.num_programs(1) - 1)
    def _():
        y_ref[...] = x_ref[...] + _rms(acc_s[...], gpost_ref[...])


def _ffn(x, g_pre, w_up, conv_w, conv_b, w_down, layer, g_post, prev, rows, sub_rows, tn):
    n, d = x.shape
    d_ff = w_down.shape[1]
    nff = d_ff // tn
    has_prev = prev is not None
    assert rows % sub_rows == 0 and (not has_prev or sub_rows == rows)
    xmap = lambda i, j: (i, 0)
    fix = lambda i, j: (0, 0)
    gate = lambda i, j: (0, j)
    val = lambda i, j: (0, j + nff)
    in_specs = [pl.BlockSpec((rows, d), xmap), pl.BlockSpec((1, d), fix),
                pl.BlockSpec((None, d, tn), lambda i, j: (layer, 0, j)),
                pl.BlockSpec((None, d, tn), lambda i, j: (layer, 0, j + nff)),
                pl.BlockSpec((CONV_W, tn), gate), pl.BlockSpec((CONV_W, tn), val),
                pl.BlockSpec((1, tn), gate), pl.BlockSpec((1, tn), val),
                pl.BlockSpec((None, tn, d), lambda i, j: (layer, j, 0)), pl.BlockSpec((1, d), fix)]
    args = [x, g_pre, w_up, w_up, conv_w, conv_w, conv_b, conv_b, w_down, g_post]
    if has_prev:
        in_specs += [pl.BlockSpec((rows, tn), lambda i, j: (i, j)),
                     pl.BlockSpec((rows, tn), lambda i, j: (i, j + nff))]
        args += [prev, prev]
        t_rows = rows
    else:
        t_rows = ROW_GROUP
    t_spec = pl.BlockSpec((t_rows, tn), lambda i, j: (i, j))
    nb = n // rows
    return pl.pallas_call(
        functools.partial(_ffn_kernel, rows=rows, sub_rows=sub_rows, has_prev=has_prev),
        grid=(nb, nff),
        in_specs=in_specs,
        out_specs=[pl.BlockSpec((rows, d), xmap), t_spec, t_spec],
        out_shape=[jax.ShapeDtypeStruct((n, d), F32),
                   jax.ShapeDtypeStruct((nb * t_rows, d_ff), F32),
                   jax.ShapeDtypeStruct((nb * t_rows, d_ff), F32)],
        scratch_shapes=[pltpu.VMEM((rows, d), BF16), pltpu.VMEM((rows, d), F32)],
        compiler_params=_cparams(("parallel", "arbitrary")),
        name="ffn",
    )(*args)


def _pick(n, prefs):
    for p in prefs:
        if n % p == 0:
            return p
    return n


def kernel(x_prompt, x_sample, cache_k, cache_v, page_table, state_hgrn, state_conv, norm_mix_pre, norm_mix_post, norm_ffn_pre, norm_ffn_post, w_in, hgrn_lb, hgrn_norm, lam_q1, lam_k1, lam_q2, lam_k2, diff_norm, gmlp_ln_g, gmlp_ln_b, gmlp_ws, gmlp_bs, w_out, w_up, conv_w, conv_b, w_down):
    bp, tp, d = x_prompt.shape
    bs, ts, _ = x_sample.shape
    depth = w_in.shape[0]
    d_ff = w_down.shape[1]
    n_pool, page = cache_k.shape[1], cache_k.shape[2]
    past = page_table.shape[1] * page
    assert ts <= ROW_GROUP - (CONV_W - 1) and ts <= CHUNK_C
    assert tp % CHUNK_C == 0 and (bs * ROW_GROUP) % CHUNK_C == 0

    lb_soft = jax.nn.softmax(hgrn_lb.astype(F32), axis=0)
    lb_all = jnp.clip(jnp.cumsum(lb_soft, axis=0) - lb_soft[0], 0.0, 1.0 - 1e-6)

    tabs_p = _rope_tables(jnp.arange(tp))
    pos_s = jnp.minimum(jnp.arange(ROW_GROUP), ts - 1) + past
    tabs_s = tuple(jnp.tile(a, (bs, 1)) for a in _rope_tables(pos_s))

    ck = jnp.transpose(cache_k, (0, 1, 3, 4, 2)).reshape(depth, n_pool, D_QK, page)
    cv = cache_v.reshape(depth, n_pool, page * H_B, DV_B)
    n_grp = _pick(page_table.shape[1], (16, 8, 4, 2))

    ns = bs * ROW_GROUP
    xp = x_prompt.reshape(bp * tp, d)
    xs = jnp.pad(x_sample, ((0, 0), (0, ROW_GROUP - ts), (0, 0))).reshape(ns, d)

    tm_p = _pick(tp, (512, 256, 128))
    tq = _pick(tp, (512, 256, 128))
    tn = _pick(d_ff, (256, 128))
    tn_s = _pick(d_ff, (1408, 256, 128))
    lc_p = _pick(tp, (CHUNK_A,))
    bs_p = _pick(lc_p, (16, ROW_GROUP))
    sub_p = _pick(tp, (512, 256, 128))

    eye_g = jnp.eye(CHUNK_C // ROW_GROUP, dtype=F32)

    w_in_b, w_out_b, w_up_b, w_down_b = (w.astype(BF16) for w in (w_in, w_out, w_up, w_down))

    outs = {k: [] for k in ("k_p", "v_p", "k_s", "v_s", "hg_p", "hg_s", "gv_s", "cv_p", "cv_s")}
    for l in range(depth):
        lam_init = 0.8 - 0.6 * math.exp(-0.3 * l)
        lam = (jnp.exp(jnp.sum(lam_q1[l].astype(F32) * lam_k1[l].astype(F32)))
               - jnp.exp(jnp.sum(lam_q2[l].astype(F32) * lam_k2[l].astype(F32)))
               + lam_init).reshape(1, 1)
        lb = lb_all[l]
        lbp = jnp.stack([jnp.log(lb), jnp.log1p(-lb), 1.0 - lb])
        g_hgrn = jnp.tile(hgrn_norm[l], H_A)[None, :]
        g_diff = diff_norm[l][None, :]
        ws = gmlp_ws[l].astype(F32)
        bsl = gmlp_bs[l].astype(F32)
        bt_p = jnp.repeat(bsl.T, CG_C, axis=1)
        ws8 = jnp.pad(ws[:, :ts, :ts], ((0, 0), (0, ROW_GROUP - ts), (0, ROW_GROUP - ts)))
        ws_s = jnp.einsum('ab,gts->gatbs', eye_g, ws8).reshape(G_C, CHUNK_C, CHUNK_C)
        bt_s = jnp.tile(jnp.repeat(bsl.T[:ROW_GROUP], CG_C, axis=1), (CHUNK_C // ROW_GROUP, 1))
        lg = gmlp_ln_g[l][None, :]
        lbn = gmlp_ln_b[l][None, :]
        cb = conv_b[l][None, :]

        oa, qq, k, v, kb, vb, uv, st = _proj(xp, norm_mix_pre[l][None, :], w_in_b, l, tabs_p, tm_p,
                                             (lbp, g_hgrn, lc_p, bs_p))
        seq = lambda a: a.reshape(bp, tp, -1)
        ob = _attn_prompt(seq(qq), seq(kb), seq(vb), lam, g_diff, 1.0 - lam_init, tq)
        x1, _ = _outproj(oa.reshape(-1, D_A), ob.reshape(-1, D_B), uv, ws, bt_p, lg, lbn,
                         w_out_b, l, xp, norm_mix_post[l][None, :], tm_p, False)
        xp, tg, tv = _ffn(x1, norm_ffn_pre[l][None, :], w_up_b, conv_w[l], cb, w_down_b, l,
                          norm_ffn_post[l][None, :], None, tp, sub_p, tn)
        outs["k_p"].append(jnp.transpose(k.reshape(bp, 2 * H_B, DK_B, tp), (0, 3, 1, 2)))
        outs["v_p"].append(v.reshape(bp, tp, H_B, DV_B))
        outs["hg_p"].append(st)
        tail = jnp.concatenate([tg, tv], axis=1).reshape(bp, ROW_GROUP, 2 * d_ff)
        outs["cv_p"].append(tail[:, ROW_GROUP - (CONV_W - 1):])

        hg, q, k, v, uv = _proj(xs, norm_mix_pre[l][None, :], w_in_b, l, tabs_s, ns)
        oa, st = _hgrn(hg.reshape(bs, ROW_GROUP, -1), lbp, g_hgrn, state_hgrn[l],
                       ROW_GROUP, ROW_GROUP, ROW_GROUP, _pick(bs, (8, 4, 2, 1)), ts)
        ob = _attn_decode(q, k, v, ck, cv, page_table, l, lam, g_diff, 1.0 - lam_init, ts, n_grp)
        x1, vn = _outproj(oa.reshape(-1, D_A), ob, uv, ws_s, bt_s, lg, lbn,
                          w_out_b, l, xs, norm_mix_post[l][None, :], ns, True)
        prev = jnp.pad(jnp.roll(state_conv[l].astype(F32), -1, axis=0),
                       ((0, 0), (ROW_GROUP - (CONV_W - 1), 0), (0, 0))).reshape(ns, 2 * d_ff)
        xs, tg, tv = _ffn(x1, norm_ffn_pre[l][None, :], w_up_b, conv_w[l], cb, w_down_b, l,
                          norm_ffn_post[l][None, :], prev, ns, ns, tn_s)
        grp = lambda a: a.reshape(bs, ROW_GROUP, -1)[:, :ts]
        outs["k_s"].append(grp(k).reshape(bs, ts, 2 * H_B, DK_B))
        outs["v_s"].append(grp(v).reshape(bs, ts, H_B, DV_B))
        outs["hg_s"].append(st)
        outs["gv_s"].append(grp(vn))
        up_s = jnp.concatenate([tg, tv], axis=1).reshape(bs, ROW_GROUP, 2 * d_ff)
        outs["cv_s"].append(up_s[:, ts - (CONV_W - 1):ts])

    y_p = xp.reshape(bp, tp, d)
    y_s = xs.reshape(bs, ROW_GROUP, d)[:, :ts]
    st = lambda key: jnp.stack(outs[key])
    return (y_p, y_s, st("k_p"), st("v_p"), st("k_s"), st("v_s"), st("hg_p"), st("hg_s"),
            st("gv_s"), st("cv_p"), st("cv_s"))
```

```python
import functools
import math

import jax
import jax.numpy as jnp
from jax import lax
from jax.experimental import pallas as pl
from jax.experimental.pallas import tpu as pltpu

F32 = jnp.float32
BF16 = jnp.bfloat16

H_A = 4
DK_A = 64
D_A = H_A * DK_A
H_B = 4
DK_B = 64
DV_B = 128
D_QK = 2 * H_B * DK_B
D_B = H_B * DV_B
ROT_DIM = DK_B // 4
ROPE_THETA = 500000.0
G_C = 4
D_C = 256
CG_C = D_C // G_C
CHUNK_C = 128
CHUNK_A = 64
CONV_W = 3
ROW_GROUP = 8
LANES = 128
VMEM_LIMIT = 56 * 1024 * 1024
NEG_INF = float("-inf")


def _cparams(sem):
    return pltpu.CompilerParams(dimension_semantics=sem, vmem_limit_bytes=VMEM_LIMIT)


def _rms(x, g, eps=1e-6):
    return x * lax.rsqrt(jnp.mean(x * x, axis=-1, keepdims=True) + eps) * g


def _proj_kernel(x_ref, g_ref, w_ref, cos_ref, sa_ref, sb_ref, *refs, q_scale, attn_bf16,
                 nt, lc, bs):
    if attn_bf16:
        (lb_ref, gh_ref, oa_ref, q_ref, k_ref, v_ref, kb_ref, vb_ref, uv_ref, st_ref,
         st_s, cum_s, kk_s, v_s) = refs
    else:
        hg_ref, q_ref, k_ref, v_ref, uv_ref = refs
    h = _rms(x_ref[...], g_ref[...]).astype(BF16)

    def mm(lo, hi):
        return jnp.dot(h, w_ref[:, lo:hi], preferred_element_type=F32)

    n_hg = 4 * D_A
    if attn_bf16:
        step = pl.program_id(0)

        @pl.when(step % nt == 0)
        def _():
            st_s[...] = jnp.zeros(st_s.shape, F32)

        hg = mm(0, n_hg)
        st = st_s[...]
        for c in range(hg.shape[0] // lc):
            part = lambda k: hg[c * lc:(c + 1) * lc, k * D_A:(k + 1) * D_A]
            oa, st = _hgrn_chunk(part(0), part(1), part(2), part(3), st, lb_ref, gh_ref,
                                 cum_s.at[c], kk_s.at[c], v_s.at[c], lc=lc, bs=bs, t_valid=lc)
            oa_ref[c * lc:(c + 1) * lc, :] = oa
        st_s[...] = st

        @pl.when(step % nt == nt - 1)
        def _():
            st_ref[0] = _state_rows(st)
    else:
        hg_ref[...] = mm(0, n_hg)
    cos = cos_ref[...]
    sa = sa_ref[...]
    sb = sb_ref[...]
    half = ROT_DIM // 2

    def rope(xx):
        return (xx * cos + pltpu.roll(xx, LANES - half, 1) * sa
                + pltpu.roll(xx, half, 1) * sb)

    first = lax.broadcasted_iota(jnp.int32, (1, LANES), 1) < DK_B
    for j in range(D_QK // LANES):
        lo = n_hg + j * LANES
        qj = rope(mm(lo, lo + LANES)) * q_scale
        lo = n_hg + D_QK + j * LANES
        kj = rope(mm(lo, lo + LANES))
        if attn_bf16:
            k_ref[j * LANES:(j + 1) * LANES, :] = kj.T
            q_ref[:, 2 * j * LANES:(2 * j + 1) * LANES] = jnp.where(first, qj, 0.0).astype(BF16)
            q_ref[:, (2 * j + 1) * LANES:(2 * j + 2) * LANES] = jnp.where(first, 0.0, qj).astype(BF16)
            kb_ref[:, j * LANES:(j + 1) * LANES] = kj.astype(BF16)
        else:
            k_ref[:, j * LANES:(j + 1) * LANES] = kj
            q_ref[:, j * LANES:(j + 1) * LANES] = qj
    lo = n_hg + 2 * D_QK
    vv = mm(lo, lo + D_B)
    if attn_bf16:
        for hh in range(H_B):
            v_ref[:, hh, :] = vv[:, hh * DV_B:(hh + 1) * DV_B]
        vb_ref[...] = vv.astype(BF16)
    else:
        v_ref[...] = vv
    uv_ref[...] = mm(lo + D_B, lo + D_B + 2 * D_C)


def _proj(x, g, w_in, layer, tabs, tm, hgrn=None):
    n, d = x.shape
    n_in = w_in.shape[2]
    cos, sa, sb = tabs
    t_len = cos.shape[0]
    nt = t_len // tm
    row = lambda i: (i, 0)
    fix = lambda i: (0, 0)
    tab = lambda i: (i % nt, 0)
    fused = hgrn is not None
    in_specs = [pl.BlockSpec((tm, d), row), pl.BlockSpec((1, d), fix),
                pl.BlockSpec((None, d, n_in), lambda i: (layer, 0, 0)),
                pl.BlockSpec((tm, LANES), tab), pl.BlockSpec((tm, LANES), tab),
                pl.BlockSpec((tm, LANES), tab)]
    args = [x, g, w_in, cos, sa, sb]
    scratch = []
    lc = bs = 0
    if fused:
        lbp, g_hgrn, lc, bs = hgrn
        in_specs += [pl.BlockSpec((3, D_A), fix), pl.BlockSpec((1, D_A), fix)]
        args += [lbp, g_hgrn]
        outs = [(D_A, F32), (2 * D_QK, BF16), None, None, (D_QK, BF16), (D_B, BF16),
                (2 * D_C, F32), None]
    else:
        outs = [(4 * D_A, F32), (D_QK, F32), (D_QK, F32), (D_B, F32), (2 * D_C, F32)]
    out_specs = [o and pl.BlockSpec((tm, o[0]), row) for o in outs]
    out_shape = [o and jax.ShapeDtypeStruct((n, o[0]), o[1]) for o in outs]
    if fused:
        nb = n // t_len
        out_specs[2] = pl.BlockSpec((None, D_QK, tm), lambda i: (i // nt, 0, i % nt))
        out_shape[2] = jax.ShapeDtypeStruct((nb, D_QK, t_len), F32)
        out_specs[3] = pl.BlockSpec((tm, H_B, DV_B), lambda i: (i, 0, 0))
        out_shape[3] = jax.ShapeDtypeStruct((n, H_B, DV_B), F32)
        out_specs[7] = pl.BlockSpec((1, D_A, DK_A), lambda i: (i // nt, 0, 0))
        out_shape[7] = jax.ShapeDtypeStruct((nb, D_A, DK_A), F32)
        scratch = [pltpu.VMEM((D_A, D_A), F32)] + [pltpu.VMEM((tm // lc, lc, D_A), F32)] * 3
    res = pl.pallas_call(
        functools.partial(_proj_kernel, q_scale=DK_B ** -0.5, attn_bf16=fused, nt=nt, lc=lc, bs=bs),
        grid=(n // tm,),
        in_specs=in_specs,
        out_specs=out_specs,
        out_shape=out_shape,
        scratch_shapes=scratch,
        compiler_params=_cparams(("arbitrary",)),
        name="proj",
    )(*args)
    if fused:
        res = list(res)
        res[7] = res[7].reshape(n // t_len, H_A, DK_A, DK_A)
    return res


def _rope_tables(pos):
    half = ROT_DIM // 2
    inv = ROPE_THETA ** (-jnp.arange(half, dtype=F32) * (2.0 / ROT_DIM))
    ang = pos.astype(F32)[:, None] * inv[None, :]
    c, s = jnp.cos(ang), jnp.sin(ang)
    t = pos.shape[0]
    rest = DK_B - ROT_DIM
    cos64 = jnp.concatenate([c, c, jnp.ones((t, rest), F32)], axis=1)
    sa64 = jnp.concatenate([-s, jnp.zeros((t, half + rest), F32)], axis=1)
    sb64 = jnp.concatenate([jnp.zeros((t, half), F32), s, jnp.zeros((t, rest), F32)], axis=1)
    rep = LANES // DK_B
    return tuple(jnp.tile(a, (1, rep)) for a in (cos64, sa64, sb64))


def _hgrn_chunk(q, z, vi, ga, st, lb_ref, g_ref, cum_s, kk_s, v_s, *, lc, bs, t_valid):
    nblk = lc // bs
    log_sig = jnp.minimum(z, 0.0) - jnp.log1p(jnp.exp(-jnp.abs(z)))
    a = lb_ref[0:1, :]
    b = lb_ref[1:2, :] + log_sig
    log_f = jnp.maximum(a, b) + jnp.log1p(jnp.exp(-jnp.abs(a - b)))
    kk = lb_ref[2:3, :] * jax.nn.sigmoid(-z)
    rows = lax.broadcasted_iota(jnp.int32, (lc, 1), 0)
    if t_valid < lc:
        log_f = jnp.where(rows < t_valid, log_f, 0.0)
        kk = jnp.where(rows < t_valid, kk, 0.0)

    cum = log_f
    d = 1
    while d < lc:
        cum = cum + jnp.where(rows >= d, pltpu.roll(cum, d, 0), 0.0)
        d *= 2

    cum_s[...] = cum
    kk_s[...] = kk
    v_s[...] = vi

    hr = lax.broadcasted_iota(jnp.int32, (D_A, D_A), 0) // DK_A
    hc = lax.broadcasted_iota(jnp.int32, (D_A, D_A), 1) // DK_A
    same_head = hr == hc
    ones_bd = same_head.astype(BF16)

    q_dec = (q * jnp.exp(cum)).astype(BF16)
    o_inter = lax.dot_general(q_dec, st.astype(BF16), (((1,), (1,)), ((), ())),
                              preferred_element_type=F32)
    o_blk = [o_inter[i * bs:(i + 1) * bs] for i in range(nblk)]

    sub = ROW_GROUP
    for i in range(nblk):
        r0 = i * bs
        pieces = []
        for s in range(bs):
            lo = r0 + (s // sub) * sub
            cs = cum_s[r0 + s:r0 + s + 1, :]
            ks = kk_s[r0 + s:r0 + s + 1, :]
            dd = cum[lo:r0 + bs] - cs
            if s % sub:
                dd = jnp.where(rows[lo:r0 + bs] >= r0 + s, dd, NEG_INF)
            pieces.append(q[lo:r0 + bs] * jnp.exp(dd) * ks)
        p_all = jnp.dot(jnp.concatenate(pieces, axis=0).astype(BF16), ones_bd,
                        preferred_element_type=F32)
        off = 0
        acc = {}
        for s in range(bs):
            lo = (s // sub) * sub
            n = bs - lo
            contrib = p_all[off:off + n] * v_s[r0 + s:r0 + s + 1, :]
            acc[lo] = contrib if lo not in acc else acc[lo] + contrib
            off += n
        tot = acc[0]
        for lo, val in acc.items():
            if lo:
                tot = tot + jnp.concatenate([jnp.zeros((lo, D_A), F32), val], axis=0)
        o_blk[i] = o_blk[i] + tot

    if nblk > 1:
        hm = (lax.broadcasted_iota(jnp.int32, (H_A * bs, D_A), 0) // bs
              == lax.broadcasted_iota(jnp.int32, (H_A * bs, D_A), 1) // DK_A)
        for j in range(nblk - 1):
            r1 = (j + 1) * bs
            aj = cum[r1 - 1:r1, :]
            k_t = kk[r1 - bs:r1] * jnp.exp(aj - cum[r1 - bs:r1])
            q_t = (q[r1:] * jnp.exp(cum[r1:] - aj)).astype(BF16)
            k_bd = jnp.where(hm, jnp.concatenate([k_t] * H_A, axis=0), 0.0).astype(BF16)
            v_bd = jnp.where(hm, jnp.concatenate([vi[r1 - bs:r1]] * H_A, axis=0), 0.0).astype(BF16)
            s2 = lax.dot_general(q_t, k_bd, (((1,), (1,)), ((), ())),
                                 preferred_element_type=F32)
            o_off = jnp.dot(s2.astype(BF16), v_bd, preferred_element_type=F32)
            for i in range(j + 1, nblk):
                o_blk[i] = o_blk[i] + o_off[(i - j - 1) * bs:(i - j) * bs]

    last = cum[lc - 1:lc, :]
    k_dec = (kk * jnp.exp(last - cum)).astype(BF16)
    upd = lax.dot_general(vi.astype(BF16), k_dec, (((0,), (0,)), ((), ())),
                          preferred_element_type=F32)
    st_new = st * jnp.exp(last) + jnp.where(same_head, upd, 0.0)

    o = jnp.concatenate(o_blk, axis=0) if nblk > 1 else o_blk[0]
    sq = o * o
    hi = sq.astype(BF16)
    lo_ = (sq - hi.astype(F32)).astype(BF16)
    ms = (jnp.dot(hi, ones_bd, preferred_element_type=F32)
          + jnp.dot(lo_, ones_bd, preferred_element_type=F32)) * (1.0 / DK_A)
    oa = o * lax.rsqrt(ms + 1e-6) * g_ref[...] * (ga * jax.nn.sigmoid(ga))
    return oa, st_new


def _state_rows(st):
    bd = st.T
    tot = bd[:, 0:DK_A]
    for h in range(1, H_A):
        tot = tot + bd[:, h * DK_A:(h + 1) * DK_A]
    return tot


def _hgrn_kernel(*refs, lc, bs, n_chunks, n_seq, t_valid, zero_init):
    if zero_init:
        hg_ref, lb_ref, g_ref, oa_ref, st_ref, st_s, cum_s, kk_s, v_s = refs
    else:
        hg_ref, lb_ref, g_ref, st0_ref, oa_ref, st_ref, st_s, cum_s, kk_s, v_s = refs
    t_idx = pl.program_id(1)

    @pl.when(t_idx == 0)
    def _():
        if zero_init:
            st_s[...] = jnp.zeros(st_s.shape, F32)
        else:
            same_head = (lax.broadcasted_iota(jnp.int32, (D_A, D_A), 0) // DK_A
                         == lax.broadcasted_iota(jnp.int32, (D_A, D_A), 1) // DK_A)
            for i in range(n_seq):
                x4 = jnp.concatenate([st0_ref[i]] * H_A, axis=1)
                st_s[i] = jnp.where(same_head, x4, 0.0).T

    for i in range(n_seq):
        for ci in range(n_chunks):
            rows = pl.ds(ci * lc, lc)
            blk = lambda k: hg_ref[i, rows, k * D_A:(k + 1) * D_A]
            oa, st_new = _hgrn_chunk(blk(0), blk(1), blk(2), blk(3), st_s[i], lb_ref, g_ref,
                                     cum_s.at[i], kk_s.at[i], v_s.at[i],
                                     lc=lc, bs=bs, t_valid=t_valid)
            oa_ref[i, rows, :] = oa
            st_s[i] = st_new

    @pl.when(t_idx == pl.num_programs(1) - 1)
    def _():
        for i in range(n_seq):
            st_ref[i] = _state_rows(st_s[i])


def _hgrn(hg, lbp, g, st0, lc, bs, tb, n_seq, t_valid):
    b, t, w = hg.shape
    zero_init = st0 is None
    smap = lambda i, c: (i, 0, 0)
    in_specs = [pl.BlockSpec((n_seq, tb, w), lambda i, c: (i, c, 0)),
                pl.BlockSpec((3, D_A), lambda i, c: (0, 0)),
                pl.BlockSpec((1, D_A), lambda i, c: (0, 0))]
    args = [hg, lbp, g]
    if not zero_init:
        in_specs.append(pl.BlockSpec((n_seq, D_A, DK_A), smap))
        args.append(st0.astype(F32).reshape(b, D_A, DK_A))
    oa, st = pl.pallas_call(
        functools.partial(_hgrn_kernel, lc=lc, bs=bs, n_chunks=tb // lc, n_seq=n_seq,
                          t_valid=t_valid, zero_init=zero_init),
        grid=(b // n_seq, t // tb),
        in_specs=in_specs,
        out_specs=[pl.BlockSpec((n_seq, tb, D_A), lambda i, c: (i, c, 0)),
                   pl.BlockSpec((n_seq, D_A, DK_A), smap)],
        out_shape=[jax.ShapeDtypeStruct((b, t, D_A), F32),
                   jax.ShapeDtypeStruct((b, D_A, DK_A), F32)],
        scratch_shapes=([pltpu.VMEM((n_seq, D_A, D_A), F32)]
                        + [pltpu.VMEM((n_seq, lc, D_A), F32)] * 3),
        compiler_params=_cparams(("parallel", "arbitrary")),
        name="hgrn",
    )(*args)
    return oa, st.reshape(b, H_A, DK_A, DK_A)


def _attn_kernel(lam_ref, q1_ref, q2_ref, k_ref, v_ref, g_ref, o_ref, *, tq, out_scale):
    t = k_ref.shape[1]
    nq = t // tq
    keep = (lax.broadcasted_iota(jnp.int32, (tq, tq), 0)
            <= lax.broadcasted_iota(jnp.int32, (tq, tq), 1))
    lam = lam_ref[...]
    for qi in range(nq):
        qsl = pl.ds(qi * tq, tq)
        heads = []
        for q_ref in (q1_ref, q2_ref):
            q = q_ref[0, qsl, :]
            m = jnp.full((1, tq), NEG_INF, F32)
            l = jnp.zeros((1, tq), F32)
            acc = jnp.zeros((DV_B, tq), F32)
            for ki in range(qi + 1):
                ksl = pl.ds(ki * tq, tq)
                st = lax.dot_general(k_ref[0, ksl, :], q, (((1,), (1,)), ((), ())),
                                     preferred_element_type=F32)
                if ki == qi:
                    st = jnp.where(keep, st, NEG_INF)
                m_new = jnp.maximum(m, jnp.max(st, axis=0, keepdims=True))
                alpha = jnp.exp(m - m_new)
                p = jnp.exp(st - m_new)
                l = alpha * l + jnp.sum(p, axis=0, keepdims=True)
                pv = lax.dot_general(v_ref[0, ksl, :], p.astype(BF16), (((0,), (0,)), ((), ())),
                                     preferred_element_type=F32)
                acc = alpha * acc + pv
                m = m_new
            heads.append(acc * (1.0 / l))
        o = heads[0] - lam * heads[1]
        o = o * lax.rsqrt(jnp.mean(o * o, axis=0, keepdims=True) + 1e-6)
        o_ref[0, qsl, :] = o.T * g_ref[...] * out_scale


def _attn_prompt(qq, kb, vb, lam, g, out_scale, tq):
    b, t, _ = kb.shape
    fix = lambda i, j: (0, 0)

    def qmap(sub):
        return lambda i, j: (i, 0, 2 * j + sub)

    kmap = lambda i, j: (i, 0, j)
    return pl.pallas_call(
        functools.partial(_attn_kernel, tq=tq, out_scale=out_scale),
        grid=(b, H_B),
        in_specs=[pl.BlockSpec((1, 1), fix),
                  pl.BlockSpec((1, t, LANES), qmap(0)),
                  pl.BlockSpec((1, t, LANES), qmap(1)),
                  pl.BlockSpec((1, t, LANES), kmap),
                  pl.BlockSpec((1, t, LANES), kmap),
                  pl.BlockSpec((1, DV_B), fix)],
        out_specs=pl.BlockSpec((1, t, LANES), kmap),
        out_shape=jax.ShapeDtypeStruct((b, t, D_B), F32),
        compiler_params=_cparams(("parallel", "parallel")),
        name="attn_prompt",
    )(lam, qq, qq, kb, vb, g)


def _decode_kernel(pt_ref, lam_ref, q_ref, kn_ref, vn_ref, *rest, n_grp, t_valid, out_scale):
    k_refs = rest[:n_grp]
    v_refs = rest[n_grp:2 * n_grp]
    g_ref, o_ref, qbd_s, m_s, l_s, acc_s = rest[2 * n_grp:]
    p_idx = pl.program_id(1)
    n_sub = 2 * H_B
    nr = n_sub * ROW_GROUP
    page = k_refs[0].shape[1]
    hrow = lax.broadcasted_iota(jnp.int32, (nr, 1), 0) // ROW_GROUP
    hcol = lax.broadcasted_iota(jnp.int32, (1, D_QK), 1) // DK_B

    @pl.when(p_idx == 0)
    def _():
        qt = jnp.concatenate([q_ref[...]] * n_sub, axis=0)
        qbd_s[...] = jnp.where(hrow == hcol, qt, 0.0).astype(BF16)
        m_s[...] = jnp.full(m_s.shape, NEG_INF, F32)
        l_s[...] = jnp.zeros(l_s.shape, F32)
        acc_s[...] = jnp.zeros(acc_s.shape, F32)

    def update(s, pv):
        m_prev = m_s[...]
        m_new = jnp.maximum(m_prev, jnp.max(s, axis=-1, keepdims=True))
        alpha = jnp.exp(m_prev - m_new)
        p = jnp.exp(s - m_new)
        l_s[...] = alpha * l_s[...] + jnp.sum(p, axis=-1, keepdims=True)
        acc_s[...] = alpha * acc_s[...] + pv(p.astype(BF16))
        m_s[...] = m_new

    qbd = qbd_s[...]
    s_pages = [jnp.dot(qbd, k_refs[i][...].astype(BF16), preferred_element_type=F32)
               for i in range(n_grp)]

    def pv_pages(p):
        tot = None
        for i in range(n_grp):
            vcat = jnp.concatenate(
                [v_refs[i][pl.ds(h, page, stride=H_B), :] for h in range(H_B)], axis=1)
            t = jnp.dot(p[:, i * page:(i + 1) * page], vcat.astype(BF16),
                        preferred_element_type=F32)
            tot = t if tot is None else tot + t
        return tot

    update(jnp.concatenate(s_pages, axis=1) if n_grp > 1 else s_pages[0], pv_pages)

    @pl.when(p_idx == pl.num_programs(1) - 1)
    def _():
        zpad = jnp.zeros((page - ROW_GROUP, D_QK), F32)
        kn = jnp.concatenate([kn_ref[...], zpad], axis=0).astype(BF16)
        vn = jnp.concatenate([vn_ref[...], zpad], axis=0).astype(BF16)
        r = lax.broadcasted_iota(jnp.int32, (nr, page), 0) % ROW_GROUP
        c = lax.broadcasted_iota(jnp.int32, (nr, page), 1)
        s_new = lax.dot_general(qbd, kn, (((1,), (1,)), ((), ())), preferred_element_type=F32)
        s_new = jnp.where((c <= r) & (c < t_valid), s_new, NEG_INF)
        update(s_new, lambda p: jnp.dot(p, vn, preferred_element_type=F32))
        lam = lam_ref[...]
        coef = jnp.where(hrow % 2 == 0, 1.0, -lam)
        vcol = lax.broadcasted_iota(jnp.int32, (1, D_B), 1) // DV_B
        contrib = jnp.where(vcol == hrow // 2, acc_s[...] / l_s[...] * coef, 0.0)
        o = contrib[0:ROW_GROUP]
        for h in range(1, n_sub):
            o = o + contrib[h * ROW_GROUP:(h + 1) * ROW_GROUP]
        for j in range(H_B):
            oj = o[:, j * DV_B:(j + 1) * DV_B]
            o_ref[:, j * DV_B:(j + 1) * DV_B] = _rms(oj, g_ref[...]) * out_scale


def _attn_decode(q, k_new, v_new, cache_kt, cache_v2, page_table, layer, lam, g, out_scale,
                 t_valid, n_grp):
    n = q.shape[0]
    nb, n_pages = page_table.shape
    page = cache_kt.shape[3]
    row = lambda i, p, pt: (i, 0)
    fix = lambda i, p, pt: (0, 0)

    def cmap(gi):
        return lambda i, p, pt: (layer, pt[i, p * n_grp + gi], 0, 0)

    nr = 2 * H_B * ROW_GROUP
    grid_spec = pltpu.PrefetchScalarGridSpec(
        num_scalar_prefetch=1,
        grid=(nb, n_pages // n_grp),
        in_specs=([pl.BlockSpec((1, 1), fix),
                   pl.BlockSpec((ROW_GROUP, D_QK), row),
                   pl.BlockSpec((ROW_GROUP, D_QK), row),
                   pl.BlockSpec((ROW_GROUP, D_B), row)]
                  + [pl.BlockSpec((None, None, D_QK, page), cmap(gi)) for gi in range(n_grp)]
                  + [pl.BlockSpec((None, None, page * H_B, DV_B), cmap(gi)) for gi in range(n_grp)]
                  + [pl.BlockSpec((1, DV_B), fix)]),
        out_specs=pl.BlockSpec((ROW_GROUP, D_B), row),
        scratch_shapes=[pltpu.VMEM((nr, D_QK), BF16), pltpu.VMEM((nr, 1), F32),
                        pltpu.VMEM((nr, 1), F32), pltpu.VMEM((nr, D_B), F32)],
    )
    return pl.pallas_call(
        functools.partial(_decode_kernel, n_grp=n_grp, t_valid=t_valid, out_scale=out_scale),
        grid_spec=grid_spec,
        out_shape=jax.ShapeDtypeStruct((n, D_B), F32),
        compiler_params=_cparams(("parallel", "arbitrary")),
        name="attn_decode",
    )(page_table, lam, q, k_new, v_new, *([cache_kt] * n_grp), *([cache_v2] * n_grp), g)


def _outproj_kernel(oa_ref, ob_ref, uv_ref, ws_ref, bt_ref, lg_ref, lb_ref, w_ref, x_ref, g_ref,
                    o_ref, *vn_out, n_chunks):
    tr = lax.broadcasted_iota(jnp.int32, (CHUNK_C, CHUNK_C), 0)
    tc = lax.broadcasted_iota(jnp.int32, (CHUNK_C, CHUNK_C), 1)
    causal = tc <= tr
    grp = lax.broadcasted_iota(jnp.int32, (1, D_C), 1) // CG_C
    wgs = [jnp.where(causal, ws_ref[gi], 0.0).astype(BF16) for gi in range(G_C)]
    ocs = []
    for c in range(n_chunks):
        sl = pl.ds(c * CHUNK_C, CHUNK_C)
        u = uv_ref[sl, 0:D_C]
        v = uv_ref[sl, D_C:2 * D_C]
        mu = jnp.mean(v, axis=-1, keepdims=True)
        vc = v - mu
        var = jnp.mean(vc * vc, axis=-1, keepdims=True)
        vn = vc * lax.rsqrt(var + 1e-5) * lg_ref[...] + lb_ref[...]
        if vn_out:
            vn_out[0][sl, :] = vn
        vnb = vn.astype(BF16)
        mixed = bt_ref[...]
        for gi in range(G_C):
            mg = jnp.dot(wgs[gi], vnb, preferred_element_type=F32)
            mixed = mixed + jnp.where(grp == gi, mg, 0.0)
        ocs.append((u * mixed).astype(BF16))
    oc = jnp.concatenate(ocs, axis=0) if n_chunks > 1 else ocs[0]
    mix = jnp.dot(oa_ref[...].astype(BF16), w_ref[0:D_A, :], preferred_element_type=F32)
    mix += jnp.dot(ob_ref[...].astype(BF16), w_ref[D_A:D_A + D_B, :],
                   preferred_element_type=F32)
    mix += jnp.dot(oc, w_ref[D_A + D_B:, :], preferred_element_type=F32)
    o_ref[...] = x_ref[...] + _rms(mix, g_ref[...])


def _outproj(oa, ob, uv, ws, bt, lg, lb, w_out, layer, x, g, tm, want_vn):
    n, d = x.shape
    row = lambda i: (i, 0)
    fix = lambda i: (0, 0)
    out_specs = [pl.BlockSpec((tm, d), row)]
    out_shape = [jax.ShapeDtypeStruct((n, d), F32)]
    if want_vn:
        out_specs.append(pl.BlockSpec((tm, D_C), row))
        out_shape.append(jax.ShapeDtypeStruct((n, D_C), F32))
    res = pl.pallas_call(
        functools.partial(_outproj_kernel, n_chunks=tm // CHUNK_C),
        grid=(n // tm,),
        in_specs=[pl.BlockSpec((tm, D_A), row), pl.BlockSpec((tm, D_B), row),
                  pl.BlockSpec((tm, 2 * D_C), row),
                  pl.BlockSpec((G_C, CHUNK_C, CHUNK_C), lambda i: (0, 0, 0)),
                  pl.BlockSpec((CHUNK_C, D_C), fix),
                  pl.BlockSpec((1, D_C), fix), pl.BlockSpec((1, D_C), fix),
                  pl.BlockSpec((None,) + w_out.shape[1:], lambda i: (layer, 0, 0)),
                  pl.BlockSpec((tm, d), row), pl.BlockSpec((1, d), fix)],
        out_specs=out_specs,
        out_shape=out_shape,
        compiler_params=_cparams(("parallel",)),
        name="outproj",
    )(oa, ob, uv, ws, bt, lg, lb, w_out, x, g)
    return res if want_vn else (res[0], None)


def _gelu_tanh(x):
    c = math.sqrt(2.0 / math.pi)
    return (0.5 * x) * (1.0 + jnp.tanh(x * (c + (c * 0.044715) * (x * x))))


def _ffn_kernel(*refs, rows, sub_rows, has_prev, n_tiles, n_ff_tiles):
    it = iter(refs)
    x_ref, gpre_ref = next(it), next(it)
    wg, wv, cwg, cwv, cbg, cbv = ([next(it) for _ in range(n_tiles)] for _ in range(6))
    wd_ref, gpost_ref = next(it), next(it)
    pg_ref, pv_ref = (next(it), next(it)) if has_prev else (None, None)
    y_ref, tg_ref, tv_ref, hn_s, acc_s = it
    j = pl.program_id(1)
    nj = pl.num_programs(1)

    @pl.when(j == 0)
    def _():
        hn_s[...] = _rms(x_ref[...], gpre_ref[...]).astype(BF16)
        acc_s[...] = jnp.zeros(acc_s.shape, F32)

    tn = wg[0].shape[1]
    n_sub = rows // sub_rows
    g8 = ROW_GROUP

    def branch(hn, w_ref, cw_ref, cb_ref, p_ref, t_ref, col, tail, last):
        up = jnp.dot(hn, w_ref[...], preferred_element_type=F32)
        if has_prev:
            ridx = lax.broadcasted_iota(jnp.int32, (rows, 1), 0)
            up = jnp.where(ridx % g8 >= g8 - (CONV_W - 1), p_ref[...], up)
            t_ref[...] = up
            sh1 = pltpu.roll(up, 1, 0)
            sh2 = pltpu.roll(up, 2, 0)
        else:
            if last:
                t_ref[:, col * tn:(col + 1) * tn] = up[sub_rows - g8:, :]
            head = jnp.concatenate([tail, up[0:g8]], axis=0)
            sh1 = jnp.concatenate([pltpu.roll(head, 1, 0)[g8:], pltpu.roll(up, 1, 0)[g8:]], axis=0)
            sh2 = jnp.concatenate([pltpu.roll(head, 2, 0)[g8:], pltpu.roll(up, 2, 0)[g8:]], axis=0)
        y = cb_ref[...] + cw_ref[0:1, :] * sh2 + cw_ref[1:2, :] * sh1 + cw_ref[2:3, :] * up
        return y, up[sub_rows - g8:, :]

    def body(tiles):
        tail_g = [jnp.zeros((g8, tn), F32)] * tiles
        tail_v = list(tail_g)
        for t in range(tiles, n_tiles):
            tg_ref[:, t * tn:(t + 1) * tn] = jnp.zeros((tg_ref.shape[0], tn), F32)
            tv_ref[:, t * tn:(t + 1) * tn] = jnp.zeros((tv_ref.shape[0], tn), F32)
        for r in range(n_sub):
            sl = pl.ds(r * sub_rows, sub_rows)
            hn = hn_s[sl, :]
            last = r == n_sub - 1
            hs = []
            for t in range(tiles):
                yg, tail_g[t] = branch(hn, wg[t], cwg[t], cbg[t], pg_ref, tg_ref, t, tail_g[t], last)
                yv, tail_v[t] = branch(hn, wv[t], cwv[t], cbv[t], pv_ref, tv_ref, t, tail_v[t], last)
                hs.append((_gelu_tanh(yg) * yv).astype(BF16))
            hcat = jnp.concatenate(hs, axis=1) if tiles > 1 else hs[0]
            acc_s[sl, :] += jnp.dot(hcat, wd_ref[0:tiles * tn, :], preferred_element_type=F32)

    if n_ff_tiles % n_tiles == 0:
        body(n_tiles)
    else:
        @pl.when(j < nj - 1)
        def _():
            body(n_tiles)

        @pl.when(j == nj - 1)
        def _():
            body(n_ff_tiles % n_tiles)

    @pl.when(j == nj - 1)
    def _():
        y_ref[...] = x_ref[...] + _rms(acc_s[...], gpost_ref[...])


def _ffn(x, g_pre, w_up, conv_w, conv_b, w_down, layer, g_post, prev, rows, sub_rows, tn, n_tiles):
    n, d = x.shape
    d_ff = w_up.shape[2] // 2
    nff = d_ff // tn
    nj = -(-nff // n_tiles)
    has_prev = prev is not None
    assert rows % sub_rows == 0 and (not has_prev or (sub_rows == rows and n_tiles == 1))
    assert w_down.shape[1] == nj * n_tiles * tn
    xmap = lambda i, j: (i, 0)
    fix = lambda i, j: (0, 0)

    def tile(t, off):
        return lambda i, j: jnp.minimum(j * n_tiles + t, nff - 1) + off

    def specs(shape, off, lead=None):
        if lead is None:
            return [pl.BlockSpec(shape, lambda i, j, f=tile(t, off): (0, f(i, j)))
                    for t in range(n_tiles)]
        return [pl.BlockSpec((None,) + shape, lambda i, j, f=tile(t, off): (lead, 0, f(i, j)))
                for t in range(n_tiles)]

    in_specs = ([pl.BlockSpec((rows, d), xmap), pl.BlockSpec((1, d), fix)]
                + specs((d, tn), 0, layer) + specs((d, tn), nff, layer)
                + specs((CONV_W, tn), 0) + specs((CONV_W, tn), nff)
                + specs((1, tn), 0) + specs((1, tn), nff)
                + [pl.BlockSpec((None, n_tiles * tn, d), lambda i, j: (layer, j, 0)),
                   pl.BlockSpec((1, d), fix)])
    args = ([x, g_pre] + [w_up] * (2 * n_tiles) + [conv_w] * (2 * n_tiles)
            + [conv_b] * (2 * n_tiles) + [w_down, g_post])
    if has_prev:
        in_specs += [pl.BlockSpec((rows, tn), lambda i, j: (i, j)),
                     pl.BlockSpec((rows, tn), lambda i, j: (i, j + nff))]
        args += [prev, prev]
        t_rows = rows
    else:
        t_rows = ROW_GROUP
    t_spec = pl.BlockSpec((t_rows, n_tiles * tn), lambda i, j: (i, j))
    nb = n // rows
    return pl.pallas_call(
        functools.partial(_ffn_kernel, rows=rows, sub_rows=sub_rows, has_prev=has_prev,
                          n_tiles=n_tiles, n_ff_tiles=nff),
        grid=(nb, nj),
        in_specs=in_specs,
        out_specs=[pl.BlockSpec((rows, d), xmap), t_spec, t_spec],
        out_shape=[jax.ShapeDtypeStruct((n, d), F32),
                   jax.ShapeDtypeStruct((nb * t_rows, nj * n_tiles * tn), F32),
                   jax.ShapeDtypeStruct((nb * t_rows, nj * n_tiles * tn), F32)],
        scratch_shapes=[pltpu.VMEM((rows, d), BF16), pltpu.VMEM((rows, d), F32)],
        compiler_params=_cparams(("parallel", "arbitrary")),
        name="ffn",
    )(*args)


def _pick(n, prefs):
    for p in prefs:
        if n % p == 0:
            return p
    return n


def kernel(x_prompt, x_sample, cache_k, cache_v, page_table, state_hgrn, state_conv, norm_mix_pre, norm_mix_post, norm_ffn_pre, norm_ffn_post, w_in, hgrn_lb, hgrn_norm, lam_q1, lam_k1, lam_q2, lam_k2, diff_norm, gmlp_ln_g, gmlp_ln_b, gmlp_ws, gmlp_bs, w_out, w_up, conv_w, conv_b, w_down):
    bp, tp, d = x_prompt.shape
    bs, ts, _ = x_sample.shape
    depth = w_in.shape[0]
    d_ff = w_down.shape[1]
    n_pool, page = cache_k.shape[1], cache_k.shape[2]
    past = page_table.shape[1] * page
    assert ts <= ROW_GROUP - (CONV_W - 1) and ts <= CHUNK_C
    assert tp % CHUNK_C == 0 and (bs * ROW_GROUP) % CHUNK_C == 0

    lb_soft = jax.nn.softmax(hgrn_lb.astype(F32), axis=0)
    lb_all = jnp.clip(jnp.cumsum(lb_soft, axis=0) - lb_soft[0], 0.0, 1.0 - 1e-6)

    tabs_p = _rope_tables(jnp.arange(tp))
    pos_s = jnp.minimum(jnp.arange(ROW_GROUP), ts - 1) + past
    tabs_s = tuple(jnp.tile(a, (bs, 1)) for a in _rope_tables(pos_s))

    ck = jnp.transpose(cache_k, (0, 1, 3, 4, 2)).reshape(depth, n_pool, D_QK, page)
    cv = cache_v.reshape(depth, n_pool, page * H_B, DV_B)
    n_grp = _pick(page_table.shape[1], (32, 16, 8, 4, 2))

    ns = bs * ROW_GROUP
    xp = x_prompt.reshape(bp * tp, d)
    xs = jnp.pad(x_sample, ((0, 0), (0, ROW_GROUP - ts), (0, 0))).reshape(ns, d)

    tm_p = _pick(tp, (512, 256, 128))
    tq = _pick(tp, (512, 256, 128))
    tn = _pick(d_ff, (256, 128))
    tn_s = _pick(d_ff, (1408, 256, 128))
    lc_p = _pick(tp, (CHUNK_A,))
    bs_p = _pick(lc_p, (16, ROW_GROUP))
    sub_p = _pick(tp, (512, 256, 128))

    eye_g = jnp.eye(CHUNK_C // ROW_GROUP, dtype=F32)

    w_in_b, w_out_b, w_up_b, w_down_b = (w.astype(BF16) for w in (w_in, w_out, w_up, w_down))
    ffn_tiles = 2
    pad_ff = -d_ff % (ffn_tiles * tn)
    w_down_p = jnp.pad(w_down_b, ((0, 0), (0, pad_ff), (0, 0)))

    outs = {k: [] for k in ("k_p", "v_p", "k_s", "v_s", "hg_p", "hg_s", "gv_s", "cv_p", "cv_s")}
    for l in range(depth):
        lam_init = 0.8 - 0.6 * math.exp(-0.3 * l)
        lam = (jnp.exp(jnp.sum(lam_q1[l].astype(F32) * lam_k1[l].astype(F32)))
               - jnp.exp(jnp.sum(lam_q2[l].astype(F32) * lam_k2[l].astype(F32)))
               + lam_init).reshape(1, 1)
        lb = lb_all[l]
        lbp = jnp.stack([jnp.log(lb), jnp.log1p(-lb), 1.0 - lb])
        g_hgrn = jnp.tile(hgrn_norm[l], H_A)[None, :]
        g_diff = diff_norm[l][None, :]
        ws = gmlp_ws[l].astype(F32)
        bsl = gmlp_bs[l].astype(F32)
        bt_p = jnp.repeat(bsl.T, CG_C, axis=1)
        ws8 = jnp.pad(ws[:, :ts, :ts], ((0, 0), (0, ROW_GROUP - ts), (0, ROW_GROUP - ts)))
        ws_s = jnp.einsum('ab,gts->gatbs', eye_g, ws8).reshape(G_C, CHUNK_C, CHUNK_C)
        bt_s = jnp.tile(jnp.repeat(bsl.T[:ROW_GROUP], CG_C, axis=1), (CHUNK_C // ROW_GROUP, 1))
        lg = gmlp_ln_g[l][None, :]
        lbn = gmlp_ln_b[l][None, :]
        cb = conv_b[l][None, :]

        oa, qq, k, v, kb, vb, uv, st = _proj(xp, norm_mix_pre[l][None, :], w_in_b, l, tabs_p, tm_p,
                                             (lbp, g_hgrn, lc_p, bs_p))
        seq = lambda a: a.reshape(bp, tp, -1)
        ob = _attn_prompt(seq(qq), seq(kb), seq(vb), lam, g_diff, 1.0 - lam_init, tq)
        x1, _ = _outproj(oa.reshape(-1, D_A), ob.reshape(-1, D_B), uv, ws, bt_p, lg, lbn,
                         w_out_b, l, xp, norm_mix_post[l][None, :], tm_p, False)
        xp, tg, tv = _ffn(x1, norm_ffn_pre[l][None, :], w_up_b, conv_w[l], cb, w_down_p, l,
                          norm_ffn_post[l][None, :], None, tp, sub_p, tn, ffn_tiles)
        outs["k_p"].append(jnp.transpose(k.reshape(bp, 2 * H_B, DK_B, tp), (0, 3, 1, 2)))
        outs["v_p"].append(v.reshape(bp, tp, H_B, DV_B))
        outs["hg_p"].append(st)
        tail = jnp.concatenate([tg[:, :d_ff], tv[:, :d_ff]], axis=1).reshape(bp, ROW_GROUP, 2 * d_ff)
        outs["cv_p"].append(tail[:, ROW_GROUP - (CONV_W - 1):])

        hg, q, k, v, uv = _proj(xs, norm_mix_pre[l][None, :], w_in_b, l, tabs_s, ns)
        oa, st = _hgrn(hg.reshape(bs, ROW_GROUP, -1), lbp, g_hgrn, state_hgrn[l],
                       ROW_GROUP, ROW_GROUP, ROW_GROUP, _pick(bs, (8, 4, 2, 1)), ts)
        ob = _attn_decode(q, k, v, ck, cv, page_table, l, lam, g_diff, 1.0 - lam_init, ts, n_grp)
        x1, vn = _outproj(oa.reshape(-1, D_A), ob, uv, ws_s, bt_s, lg, lbn,
                          w_out_b, l, xs, norm_mix_post[l][None, :], ns, True)
        prev = jnp.pad(jnp.roll(state_conv[l].astype(F32), -1, axis=0),
                       ((0, 0), (ROW_GROUP - (CONV_W - 1), 0), (0, 0))).reshape(ns, 2 * d_ff)
        xs, tg, tv = _ffn(x1, norm_ffn_pre[l][None, :], w_up_b, conv_w[l], cb, w_down_b, l,
                          norm_ffn_post[l][None, :], prev, ns, ns, tn_s, 1)
        grp = lambda a: a.reshape(bs, ROW_GROUP, -1)[:, :ts]
        outs["k_s"].append(grp(k).reshape(bs, ts, 2 * H_B, DK_B))
        outs["v_s"].append(grp(v).reshape(bs, ts, H_B, DV_B))
        outs["hg_s"].append(st)
        outs["gv_s"].append(grp(vn))
        up_s = jnp.concatenate([tg, tv], axis=1).reshape(bs, ROW_GROUP, 2 * d_ff)
        outs["cv_s"].append(up_s[:, ts - (CONV_W - 1):ts])

    y_p = xp.reshape(bp, tp, d)
    y_s = xs.reshape(bs, ROW_GROUP, d)[:, :ts]
    st = lambda key: jnp.stack(outs[key])
    return (y_p, y_s, st("k_p"), st("v_p"), st("k_s"), st("v_s"), st("hg_p"), st("hg_s"),
            st("gv_s"), st("cv_p"), st("cv_s"))
```

```python
import functools
import math

import jax
import jax.numpy as jnp
from jax import lax
from jax.experimental import pallas as pl
from jax.experimental.pallas import tpu as pltpu

F32 = jnp.float32
BF16 = jnp.bfloat16

H_A = 4
DK_A = 64
D_A = H_A * DK_A
H_B = 4
DK_B = 64
DV_B = 128
D_QK = 2 * H_B * DK_B
D_B = H_B * DV_B
ROT_DIM = DK_B // 4
ROPE_THETA = 500000.0
G_C = 4
D_C = 256
CG_C = D_C // G_C
CHUNK_C = 128
CHUNK_A = 64
CONV_W = 3
ROW_GROUP = 8
LANES = 128
VMEM_LIMIT = 58 * 1024 * 1024
NEG_INF = float("-inf")


def _cparams(sem):
    return pltpu.CompilerParams(dimension_semantics=sem, vmem_limit_bytes=VMEM_LIMIT)


def _rms(x, g, eps=1e-6):
    return x * lax.rsqrt(jnp.mean(x * x, axis=-1, keepdims=True) + eps) * g


def _proj_kernel(x_ref, g_ref, w_ref, cos_ref, sa_ref, sb_ref, *refs, q_scale, attn_bf16,
                 nt, lc, bs):
    if attn_bf16:
        (lb_ref, gh_ref, oa_ref, q_ref, k_ref, v_ref, kb_ref, vb_ref, uv_ref, st_ref,
         st_s, cum_s, kk_s, v_s) = refs
    else:
        hg_ref, q_ref, k_ref, v_ref, uv_ref = refs
    h = _rms(x_ref[...], g_ref[...]).astype(BF16)

    def mm(lo, hi):
        return jnp.dot(h, w_ref[:, lo:hi], preferred_element_type=F32)

    n_hg = 4 * D_A
    if attn_bf16:
        step = pl.program_id(0)

        @pl.when(step % nt == 0)
        def _():
            st_s[...] = jnp.zeros(st_s.shape, F32)

        hg = mm(0, n_hg)
        st = st_s[...]
        for c in range(hg.shape[0] // lc):
            part = lambda k: hg[c * lc:(c + 1) * lc, k * D_A:(k + 1) * D_A]
            oa, st = _hgrn_chunk(part(0), part(1), part(2), part(3), st, lb_ref, gh_ref,
                                 cum_s.at[c], kk_s.at[c], v_s.at[c], lc=lc, bs=bs, t_valid=lc)
            oa_ref[c * lc:(c + 1) * lc, :] = oa
        st_s[...] = st

        @pl.when(step % nt == nt - 1)
        def _():
            st_ref[0] = _state_rows(st)
    else:
        hg_ref[...] = mm(0, n_hg)
    cos = cos_ref[...]
    sa = sa_ref[...]
    sb = sb_ref[...]
    half = ROT_DIM // 2

    def rope(xx):
        return (xx * cos + pltpu.roll(xx, LANES - half, 1) * sa
                + pltpu.roll(xx, half, 1) * sb)

    first = lax.broadcasted_iota(jnp.int32, (1, LANES), 1) < DK_B
    for j in range(D_QK // LANES):
        lo = n_hg + j * LANES
        qj = rope(mm(lo, lo + LANES)) * q_scale
        lo = n_hg + D_QK + j * LANES
        kj = rope(mm(lo, lo + LANES))
        if attn_bf16:
            k_ref[j * LANES:(j + 1) * LANES, :] = kj.T
            q_ref[:, 2 * j * LANES:(2 * j + 1) * LANES] = jnp.where(first, qj, 0.0).astype(BF16)
            q_ref[:, (2 * j + 1) * LANES:(2 * j + 2) * LANES] = jnp.where(first, 0.0, qj).astype(BF16)
            kb_ref[:, j * LANES:(j + 1) * LANES] = kj.astype(BF16)
        else:
            k_ref[:, j * LANES:(j + 1) * LANES] = kj
            q_ref[:, j * LANES:(j + 1) * LANES] = qj
    lo = n_hg + 2 * D_QK
    vv = mm(lo, lo + D_B)
    if attn_bf16:
        for hh in range(H_B):
            v_ref[:, hh, :] = vv[:, hh * DV_B:(hh + 1) * DV_B]
        vb_ref[...] = vv.astype(BF16)
    else:
        v_ref[...] = vv
    uv_ref[...] = mm(lo + D_B, lo + D_B + 2 * D_C)


def _proj(x, g, w_in, layer, tabs, tm, hgrn=None):
    n, d = x.shape
    n_in = w_in.shape[2]
    cos, sa, sb = tabs
    t_len = cos.shape[0]
    nt = t_len // tm
    row = lambda i: (i, 0)
    fix = lambda i: (0, 0)
    tab = lambda i: (i % nt, 0)
    fused = hgrn is not None
    in_specs = [pl.BlockSpec((tm, d), row), pl.BlockSpec((1, d), fix),
                pl.BlockSpec((None, d, n_in), lambda i: (layer, 0, 0)),
                pl.BlockSpec((tm, LANES), tab), pl.BlockSpec((tm, LANES), tab),
                pl.BlockSpec((tm, LANES), tab)]
    args = [x, g, w_in, cos, sa, sb]
    scratch = []
    lc = bs = 0
    if fused:
        lbp, g_hgrn, lc, bs = hgrn
        in_specs += [pl.BlockSpec((3, D_A), fix), pl.BlockSpec((1, D_A), fix)]
        args += [lbp, g_hgrn]
        outs = [(D_A, F32), (2 * D_QK, BF16), None, None, (D_QK, BF16), (D_B, BF16),
                (2 * D_C, F32), None]
    else:
        outs = [(4 * D_A, F32), (D_QK, F32), (D_QK, F32), (D_B, F32), (2 * D_C, F32)]
    out_specs = [o and pl.BlockSpec((tm, o[0]), row) for o in outs]
    out_shape = [o and jax.ShapeDtypeStruct((n, o[0]), o[1]) for o in outs]
    if fused:
        nb = n // t_len
        out_specs[2] = pl.BlockSpec((None, D_QK, tm), lambda i: (i // nt, 0, i % nt))
        out_shape[2] = jax.ShapeDtypeStruct((nb, D_QK, t_len), F32)
        out_specs[3] = pl.BlockSpec((tm, H_B, DV_B), lambda i: (i, 0, 0))
        out_shape[3] = jax.ShapeDtypeStruct((n, H_B, DV_B), F32)
        out_specs[7] = pl.BlockSpec((1, D_A, DK_A), lambda i: (i // nt, 0, 0))
        out_shape[7] = jax.ShapeDtypeStruct((nb, D_A, DK_A), F32)
        scratch = [pltpu.VMEM((D_A, D_A), F32)] + [pltpu.VMEM((tm // lc, lc, D_A), F32)] * 3
    res = pl.pallas_call(
        functools.partial(_proj_kernel, q_scale=DK_B ** -0.5, attn_bf16=fused, nt=nt, lc=lc, bs=bs),
        grid=(n // tm,),
        in_specs=in_specs,
        out_specs=out_specs,
        out_shape=out_shape,
        scratch_shapes=scratch,
        compiler_params=_cparams(("arbitrary",)),
        name="proj",
    )(*args)
    if fused:
        res = list(res)
        res[7] = res[7].reshape(n // t_len, H_A, DK_A, DK_A)
    return res


def _rope_tables(pos):
    half = ROT_DIM // 2
    inv = ROPE_THETA ** (-jnp.arange(half, dtype=F32) * (2.0 / ROT_DIM))
    ang = pos.astype(F32)[:, None] * inv[None, :]
    c, s = jnp.cos(ang), jnp.sin(ang)
    t = pos.shape[0]
    rest = DK_B - ROT_DIM
    cos64 = jnp.concatenate([c, c, jnp.ones((t, rest), F32)], axis=1)
    sa64 = jnp.concatenate([-s, jnp.zeros((t, half + rest), F32)], axis=1)
    sb64 = jnp.concatenate([jnp.zeros((t, half), F32), s, jnp.zeros((t, rest), F32)], axis=1)
    rep = LANES // DK_B
    return tuple(jnp.tile(a, (1, rep)) for a in (cos64, sa64, sb64))


def _hgrn_chunk(q, z, vi, ga, st, lb_ref, g_ref, cum_s, kk_s, v_s, *, lc, bs, t_valid):
    nblk = lc // bs
    log_sig = jnp.minimum(z, 0.0) - jnp.log1p(jnp.exp(-jnp.abs(z)))
    a = lb_ref[0:1, :]
    b = lb_ref[1:2, :] + log_sig
    log_f = jnp.maximum(a, b) + jnp.log1p(jnp.exp(-jnp.abs(a - b)))
    kk = lb_ref[2:3, :] * jax.nn.sigmoid(-z)
    rows = lax.broadcasted_iota(jnp.int32, (lc, 1), 0)
    if t_valid < lc:
        log_f = jnp.where(rows < t_valid, log_f, 0.0)
        kk = jnp.where(rows < t_valid, kk, 0.0)

    cum = log_f
    d = 1
    while d < lc:
        cum = cum + jnp.where(rows >= d, pltpu.roll(cum, d, 0), 0.0)
        d *= 2

    cum_s[...] = cum
    kk_s[...] = kk
    v_s[...] = vi

    hr = lax.broadcasted_iota(jnp.int32, (D_A, D_A), 0) // DK_A
    hc = lax.broadcasted_iota(jnp.int32, (D_A, D_A), 1) // DK_A
    same_head = hr == hc
    ones_bd = same_head.astype(BF16)

    q_dec = (q * jnp.exp(cum)).astype(BF16)
    o_inter = lax.dot_general(q_dec, st.astype(BF16), (((1,), (1,)), ((), ())),
                              preferred_element_type=F32)
    o_blk = [o_inter[i * bs:(i + 1) * bs] for i in range(nblk)]

    sub = ROW_GROUP
    for i in range(nblk):
        r0 = i * bs
        pieces = []
        for s in range(bs):
            lo = r0 + (s // sub) * sub
            cs = cum_s[r0 + s:r0 + s + 1, :]
            ks = kk_s[r0 + s:r0 + s + 1, :]
            dd = cum[lo:r0 + bs] - cs
            if s % sub:
                dd = jnp.where(rows[lo:r0 + bs] >= r0 + s, dd, NEG_INF)
            pieces.append(q[lo:r0 + bs] * jnp.exp(dd) * ks)
        p_all = jnp.dot(jnp.concatenate(pieces, axis=0).astype(BF16), ones_bd,
                        preferred_element_type=F32)
        off = 0
        acc = {}
        for s in range(bs):
            lo = (s // sub) * sub
            n = bs - lo
            contrib = p_all[off:off + n] * v_s[r0 + s:r0 + s + 1, :]
            acc[lo] = contrib if lo not in acc else acc[lo] + contrib
            off += n
        tot = acc[0]
        for lo, val in acc.items():
            if lo:
                tot = tot + jnp.concatenate([jnp.zeros((lo, D_A), F32), val], axis=0)
        o_blk[i] = o_blk[i] + tot

    if nblk > 1:
        hm = (lax.broadcasted_iota(jnp.int32, (H_A * bs, D_A), 0) // bs
              == lax.broadcasted_iota(jnp.int32, (H_A * bs, D_A), 1) // DK_A)
        for j in range(nblk - 1):
            r1 = (j + 1) * bs
            aj = cum[r1 - 1:r1, :]
            k_t = kk[r1 - bs:r1] * jnp.exp(aj - cum[r1 - bs:r1])
            q_t = (q[r1:] * jnp.exp(cum[r1:] - aj)).astype(BF16)
            k_bd = jnp.where(hm, jnp.concatenate([k_t] * H_A, axis=0), 0.0).astype(BF16)
            v_bd = jnp.where(hm, jnp.concatenate([vi[r1 - bs:r1]] * H_A, axis=0), 0.0).astype(BF16)
            s2 = lax.dot_general(q_t, k_bd, (((1,), (1,)), ((), ())),
                                 preferred_element_type=F32)
            o_off = jnp.dot(s2.astype(BF16), v_bd, preferred_element_type=F32)
            for i in range(j + 1, nblk):
                o_blk[i] = o_blk[i] + o_off[(i - j - 1) * bs:(i - j) * bs]

    last = cum[lc - 1:lc, :]
    k_dec = (kk * jnp.exp(last - cum)).astype(BF16)
    upd = lax.dot_general(vi.astype(BF16), k_dec, (((0,), (0,)), ((), ())),
                          preferred_element_type=F32)
    st_new = st * jnp.exp(last) + jnp.where(same_head, upd, 0.0)

    o = jnp.concatenate(o_blk, axis=0) if nblk > 1 else o_blk[0]
    sq = o * o
    hi = sq.astype(BF16)
    lo_ = (sq - hi.astype(F32)).astype(BF16)
    ms = (jnp.dot(hi, ones_bd, preferred_element_type=F32)
          + jnp.dot(lo_, ones_bd, preferred_element_type=F32)) * (1.0 / DK_A)
    oa = o * lax.rsqrt(ms + 1e-6) * g_ref[...] * (ga * jax.nn.sigmoid(ga))
    return oa, st_new


def _state_rows(st):
    bd = st.T
    tot = bd[:, 0:DK_A]
    for h in range(1, H_A):
        tot = tot + bd[:, h * DK_A:(h + 1) * DK_A]
    return tot


def _hgrn_kernel(hg_ref, lb_ref, g_ref, st0_ref, oa_ref, st_ref, st_s, cum_s, kk_s, v_s,
                 *, lc, bs, n_chunks, n_seq, t_valid):
    t_idx = pl.program_id(1)

    @pl.when(t_idx == 0)
    def _():
        same_head = (lax.broadcasted_iota(jnp.int32, (D_A, D_A), 0) // DK_A
                     == lax.broadcasted_iota(jnp.int32, (D_A, D_A), 1) // DK_A)
        for i in range(n_seq):
            x4 = jnp.concatenate([st0_ref[i]] * H_A, axis=1)
            st_s[i] = jnp.where(same_head, x4, 0.0).T

    for i in range(n_seq):
        for ci in range(n_chunks):
            rows = pl.ds(ci * lc, lc)
            blk = lambda k: hg_ref[i, rows, k * D_A:(k + 1) * D_A]
            oa, st_new = _hgrn_chunk(blk(0), blk(1), blk(2), blk(3), st_s[i], lb_ref, g_ref,
                                     cum_s.at[i], kk_s.at[i], v_s.at[i],
                                     lc=lc, bs=bs, t_valid=t_valid)
            oa_ref[i, rows, :] = oa
            st_s[i] = st_new

    @pl.when(t_idx == pl.num_programs(1) - 1)
    def _():
        for i in range(n_seq):
            st_ref[i] = _state_rows(st_s[i])


def _hgrn(hg, lbp, g, st0, lc, bs, tb, n_seq, t_valid):
    b, t, w = hg.shape
    smap = lambda i, c: (i, 0, 0)
    in_specs = [pl.BlockSpec((n_seq, tb, w), lambda i, c: (i, c, 0)),
                pl.BlockSpec((3, D_A), lambda i, c: (0, 0)),
                pl.BlockSpec((1, D_A), lambda i, c: (0, 0)),
                pl.BlockSpec((n_seq, D_A, DK_A), smap)]
    args = [hg, lbp, g, st0.astype(F32).reshape(b, D_A, DK_A)]
    oa, st = pl.pallas_call(
        functools.partial(_hgrn_kernel, lc=lc, bs=bs, n_chunks=tb // lc, n_seq=n_seq,
                          t_valid=t_valid),
        grid=(b // n_seq, t // tb),
        in_specs=in_specs,
        out_specs=[pl.BlockSpec((n_seq, tb, D_A), lambda i, c: (i, c, 0)),
                   pl.BlockSpec((n_seq, D_A, DK_A), smap)],
        out_shape=[jax.ShapeDtypeStruct((b, t, D_A), F32),
                   jax.ShapeDtypeStruct((b, D_A, DK_A), F32)],
        scratch_shapes=([pltpu.VMEM((n_seq, D_A, D_A), F32)]
                        + [pltpu.VMEM((n_seq, lc, D_A), F32)] * 3),
        compiler_params=_cparams(("parallel", "arbitrary")),
        name="hgrn",
    )(*args)
    return oa, st.reshape(b, H_A, DK_A, DK_A)


def _attn_kernel(lam_ref, q1_ref, q2_ref, k_ref, v_ref, g_ref, o_ref, *, tq, out_scale):
    t = k_ref.shape[1]
    nq = t // tq
    keep = (lax.broadcasted_iota(jnp.int32, (tq, tq), 0)
            <= lax.broadcasted_iota(jnp.int32, (tq, tq), 1))
    lam = lam_ref[...]
    for qi in range(nq):
        qsl = pl.ds(qi * tq, tq)
        heads = []
        for q_ref in (q1_ref, q2_ref):
            q = q_ref[0, qsl, :]
            m = jnp.full((1, tq), NEG_INF, F32)
            l = jnp.zeros((1, tq), F32)
            acc = jnp.zeros((DV_B, tq), F32)
            for ki in range(qi + 1):
                ksl = pl.ds(ki * tq, tq)
                st = lax.dot_general(k_ref[0, ksl, :], q, (((1,), (1,)), ((), ())),
                                     preferred_element_type=F32)
                if ki == qi:
                    st = jnp.where(keep, st, NEG_INF)
                m_new = jnp.maximum(m, jnp.max(st, axis=0, keepdims=True))
                alpha = jnp.exp(m - m_new)
                p = jnp.exp(st - m_new)
                l = alpha * l + jnp.sum(p, axis=0, keepdims=True)
                pv = lax.dot_general(v_ref[0, ksl, :], p.astype(BF16), (((0,), (0,)), ((), ())),
                                     preferred_element_type=F32)
                acc = alpha * acc + pv
                m = m_new
            heads.append(acc * (1.0 / l))
        o = heads[0] - lam * heads[1]
        o = o * lax.rsqrt(jnp.mean(o * o, axis=0, keepdims=True) + 1e-6)
        o_ref[0, qsl, :] = o.T * g_ref[...] * out_scale


def _attn_prompt(qq, kb, vb, lam, g, out_scale, tq):
    b, t, _ = kb.shape
    fix = lambda i, j: (0, 0)

    def qmap(sub):
        return lambda i, j: (i, 0, 2 * j + sub)

    kmap = lambda i, j: (i, 0, j)
    return pl.pallas_call(
        functools.partial(_attn_kernel, tq=tq, out_scale=out_scale),
        grid=(b, H_B),
        in_specs=[pl.BlockSpec((1, 1), fix),
                  pl.BlockSpec((1, t, LANES), qmap(0)),
                  pl.BlockSpec((1, t, LANES), qmap(1)),
                  pl.BlockSpec((1, t, LANES), kmap),
                  pl.BlockSpec((1, t, LANES), kmap),
                  pl.BlockSpec((1, DV_B), fix)],
        out_specs=pl.BlockSpec((1, t, LANES), kmap),
        out_shape=jax.ShapeDtypeStruct((b, t, D_B), F32),
        compiler_params=_cparams(("parallel", "parallel")),
        name="attn_prompt",
    )(lam, qq, qq, kb, vb, g)


def _decode_kernel(pt_ref, lam_ref, q_ref, kn_ref, vn_ref, *rest, n_grp, t_valid, out_scale):
    k_refs = rest[:n_grp]
    v_refs = rest[n_grp:2 * n_grp]
    g_ref, o_ref, qbd_s, m_s, l_s, acc_s = rest[2 * n_grp:]
    p_idx = pl.program_id(1)
    n_sub = 2 * H_B
    nr = n_sub * ROW_GROUP
    page = k_refs[0].shape[1]
    hrow = lax.broadcasted_iota(jnp.int32, (nr, 1), 0) // ROW_GROUP
    hcol = lax.broadcasted_iota(jnp.int32, (1, D_QK), 1) // DK_B

    @pl.when(p_idx == 0)
    def _():
        qt = jnp.concatenate([q_ref[...]] * n_sub, axis=0)
        qbd_s[...] = jnp.where(hrow == hcol, qt, 0.0).astype(BF16)
        m_s[...] = jnp.full(m_s.shape, NEG_INF, F32)
        l_s[...] = jnp.zeros(l_s.shape, F32)
        acc_s[...] = jnp.zeros(acc_s.shape, F32)

    def update(s, pv):
        m_prev = m_s[...]
        m_new = jnp.maximum(m_prev, jnp.max(s, axis=-1, keepdims=True))
        alpha = jnp.exp(m_prev - m_new)
        p = jnp.exp(s - m_new)
        l_s[...] = alpha * l_s[...] + jnp.sum(p, axis=-1, keepdims=True)
        acc_s[...] = alpha * acc_s[...] + pv(p.astype(BF16))
        m_s[...] = m_new

    qbd = qbd_s[...]
    s_pages = [jnp.dot(qbd, k_refs[i][...].astype(BF16), preferred_element_type=F32)
               for i in range(n_grp)]

    def pv_pages(p):
        tot = None
        for i in range(n_grp):
            vcat = jnp.concatenate(
                [v_refs[i][pl.ds(h, page, stride=H_B), :] for h in range(H_B)], axis=1)
            t = jnp.dot(p[:, i * page:(i + 1) * page], vcat.astype(BF16),
                        preferred_element_type=F32)
            tot = t if tot is None else tot + t
        return tot

    update(jnp.concatenate(s_pages, axis=1) if n_grp > 1 else s_pages[0], pv_pages)

    @pl.when(p_idx == pl.num_programs(1) - 1)
    def _():
        zpad = jnp.zeros((page - ROW_GROUP, D_QK), F32)
        kn = jnp.concatenate([kn_ref[...], zpad], axis=0).astype(BF16)
        vn = jnp.concatenate([vn_ref[...], zpad], axis=0).astype(BF16)
        r = lax.broadcasted_iota(jnp.int32, (nr, page), 0) % ROW_GROUP
        c = lax.broadcasted_iota(jnp.int32, (nr, page), 1)
        s_new = lax.dot_general(qbd, kn, (((1,), (1,)), ((), ())), preferred_element_type=F32)
        s_new = jnp.where((c <= r) & (c < t_valid), s_new, NEG_INF)
        update(s_new, lambda p: jnp.dot(p, vn, preferred_element_type=F32))
        lam = lam_ref[...]
        coef = jnp.where(hrow % 2 == 0, 1.0, -lam)
        vcol = lax.broadcasted_iota(jnp.int32, (1, D_B), 1) // DV_B
        contrib = jnp.where(vcol == hrow // 2, acc_s[...] / l_s[...] * coef, 0.0)
        o = contrib[0:ROW_GROUP]
        for h in range(1, n_sub):
            o = o + contrib[h * ROW_GROUP:(h + 1) * ROW_GROUP]
        for j in range(H_B):
            oj = o[:, j * DV_B:(j + 1) * DV_B]
            o_ref[:, j * DV_B:(j + 1) * DV_B] = _rms(oj, g_ref[...]) * out_scale


def _attn_decode(q, k_new, v_new, cache_kt, cache_v2, page_table, layer, lam, g, out_scale,
                 t_valid, n_grp):
    n = q.shape[0]
    nb, n_pages = page_table.shape
    page = cache_kt.shape[3]
    row = lambda i, p, pt: (i, 0)
    fix = lambda i, p, pt: (0, 0)

    def cmap(gi):
        return lambda i, p, pt: (layer, pt[i, p * n_grp + gi], 0, 0)

    nr = 2 * H_B * ROW_GROUP
    grid_spec = pltpu.PrefetchScalarGridSpec(
        num_scalar_prefetch=1,
        grid=(nb, n_pages // n_grp),
        in_specs=([pl.BlockSpec((1, 1), fix),
                   pl.BlockSpec((ROW_GROUP, D_QK), row),
                   pl.BlockSpec((ROW_GROUP, D_QK), row),
                   pl.BlockSpec((ROW_GROUP, D_B), row)]
                  + [pl.BlockSpec((None, None, D_QK, page), cmap(gi)) for gi in range(n_grp)]
                  + [pl.BlockSpec((None, None, page * H_B, DV_B), cmap(gi)) for gi in range(n_grp)]
                  + [pl.BlockSpec((1, DV_B), fix)]),
        out_specs=pl.BlockSpec((ROW_GROUP, D_B), row),
        scratch_shapes=[pltpu.VMEM((nr, D_QK), BF16), pltpu.VMEM((nr, 1), F32),
                        pltpu.VMEM((nr, 1), F32), pltpu.VMEM((nr, D_B), F32)],
    )
    return pl.pallas_call(
        functools.partial(_decode_kernel, n_grp=n_grp, t_valid=t_valid, out_scale=out_scale),
        grid_spec=grid_spec,
        out_shape=jax.ShapeDtypeStruct((n, D_B), F32),
        compiler_params=_cparams(("parallel", "arbitrary")),
        name="attn_decode",
    )(page_table, lam, q, k_new, v_new, *([cache_kt] * n_grp), *([cache_v2] * n_grp), g)


def _outproj_kernel(oa_ref, ob_ref, uv_ref, ws_ref, bt_ref, lg_ref, lb_ref, w_ref, x_ref, g_ref,
                    o_ref, *vn_out, n_chunks):
    tr = lax.broadcasted_iota(jnp.int32, (CHUNK_C, CHUNK_C), 0)
    tc = lax.broadcasted_iota(jnp.int32, (CHUNK_C, CHUNK_C), 1)
    causal = tc <= tr
    grp = lax.broadcasted_iota(jnp.int32, (1, D_C), 1) // CG_C
    wgs = [jnp.where(causal, ws_ref[gi], 0.0).astype(BF16) for gi in range(G_C)]
    ocs = []
    for c in range(n_chunks):
        sl = pl.ds(c * CHUNK_C, CHUNK_C)
        u = uv_ref[sl, 0:D_C]
        v = uv_ref[sl, D_C:2 * D_C]
        mu = jnp.mean(v, axis=-1, keepdims=True)
        vc = v - mu
        var = jnp.mean(vc * vc, axis=-1, keepdims=True)
        vn = vc * lax.rsqrt(var + 1e-5) * lg_ref[...] + lb_ref[...]
        if vn_out:
            vn_out[0][sl, :] = vn
        vnb = vn.astype(BF16)
        mixed = bt_ref[...]
        for gi in range(G_C):
            mg = jnp.dot(wgs[gi], vnb, preferred_element_type=F32)
            mixed = mixed + jnp.where(grp == gi, mg, 0.0)
        ocs.append((u * mixed).astype(BF16))
    oc = jnp.concatenate(ocs, axis=0) if n_chunks > 1 else ocs[0]
    mix = jnp.dot(oa_ref[...].astype(BF16), w_ref[0:D_A, :], preferred_element_type=F32)
    mix += jnp.dot(ob_ref[...].astype(BF16), w_ref[D_A:D_A + D_B, :],
                   preferred_element_type=F32)
    mix += jnp.dot(oc, w_ref[D_A + D_B:, :], preferred_element_type=F32)
    o_ref[...] = x_ref[...] + _rms(mix, g_ref[...])


def _outproj(oa, ob, uv, ws, bt, lg, lb, w_out, layer, x, g, tm, want_vn):
    n, d = x.shape
    row = lambda i: (i, 0)
    fix = lambda i: (0, 0)
    out_specs = [pl.BlockSpec((tm, d), row)]
    out_shape = [jax.ShapeDtypeStruct((n, d), F32)]
    if want_vn:
        out_specs.append(pl.BlockSpec((tm, D_C), row))
        out_shape.append(jax.ShapeDtypeStruct((n, D_C), F32))
    res = pl.pallas_call(
        functools.partial(_outproj_kernel, n_chunks=tm // CHUNK_C),
        grid=(n // tm,),
        in_specs=[pl.BlockSpec((tm, D_A), row), pl.BlockSpec((tm, D_B), row),
                  pl.BlockSpec((tm, 2 * D_C), row),
                  pl.BlockSpec((G_C, CHUNK_C, CHUNK_C), lambda i: (0, 0, 0)),
                  pl.BlockSpec((CHUNK_C, D_C), fix),
                  pl.BlockSpec((1, D_C), fix), pl.BlockSpec((1, D_C), fix),
                  pl.BlockSpec((None,) + w_out.shape[1:], lambda i: (layer, 0, 0)),
                  pl.BlockSpec((tm, d), row), pl.BlockSpec((1, d), fix)],
        out_specs=out_specs,
        out_shape=out_shape,
        compiler_params=_cparams(("parallel",)),
        name="outproj",
    )(oa, ob, uv, ws, bt, lg, lb, w_out, x, g)
    return res if want_vn else (res[0], None)


def _gelu_tanh(x):
    c = math.sqrt(2.0 / math.pi)
    return (0.5 * x) * (1.0 + jnp.tanh(x * (c + (c * 0.044715) * (x * x))))


def _ffn_kernel(*refs, rows, sub_rows, has_prev, n_tiles, n_ff_tiles):
    it = iter(refs)
    x_ref, gpre_ref = next(it), next(it)
    wg, wv, cwg, cwv, cbg, cbv = ([next(it) for _ in range(n_tiles)] for _ in range(6))
    wd_ref, gpost_ref = next(it), next(it)
    pg_ref, pv_ref = (next(it), next(it)) if has_prev else (None, None)
    y_ref, tg_ref, tv_ref, hn_s, acc_s = it
    j = pl.program_id(1)
    nj = pl.num_programs(1)

    @pl.when(j == 0)
    def _():
        hn_s[...] = _rms(x_ref[...], gpre_ref[...]).astype(BF16)
        acc_s[...] = jnp.zeros(acc_s.shape, F32)

    tn = wg[0].shape[1]
    n_sub = rows // sub_rows
    g8 = ROW_GROUP

    def branch(hn, w_ref, cw_ref, cb_ref, p_ref, t_ref, col, tail, last):
        up = jnp.dot(hn, w_ref[...], preferred_element_type=F32)
        if has_prev:
            ridx = lax.broadcasted_iota(jnp.int32, (rows, 1), 0)
            up = jnp.where(ridx % g8 >= g8 - (CONV_W - 1), p_ref[...], up)
            t_ref[...] = up
            sh1 = pltpu.roll(up, 1, 0)
            sh2 = pltpu.roll(up, 2, 0)
        else:
            if last:
                t_ref[:, col * tn:(col + 1) * tn] = up[sub_rows - g8:, :]
            head = jnp.concatenate([tail, up[0:g8]], axis=0)
            sh1 = jnp.concatenate([pltpu.roll(head, 1, 0)[g8:], pltpu.roll(up, 1, 0)[g8:]], axis=0)
            sh2 = jnp.concatenate([pltpu.roll(head, 2, 0)[g8:], pltpu.roll(up, 2, 0)[g8:]], axis=0)
        y = cb_ref[...] + cw_ref[0:1, :] * sh2 + cw_ref[1:2, :] * sh1 + cw_ref[2:3, :] * up
        return y, up[sub_rows - g8:, :]

    def body(tiles):
        tail_g = [jnp.zeros((g8, tn), F32)] * tiles
        tail_v = list(tail_g)
        for t in range(tiles, n_tiles):
            tg_ref[:, t * tn:(t + 1) * tn] = jnp.zeros((tg_ref.shape[0], tn), F32)
            tv_ref[:, t * tn:(t + 1) * tn] = jnp.zeros((tv_ref.shape[0], tn), F32)
        for r in range(n_sub):
            sl = pl.ds(r * sub_rows, sub_rows)
            hn = hn_s[sl, :]
            last = r == n_sub - 1
            hs = []
            for t in range(tiles):
                yg, tail_g[t] = branch(hn, wg[t], cwg[t], cbg[t], pg_ref, tg_ref, t, tail_g[t], last)
                yv, tail_v[t] = branch(hn, wv[t], cwv[t], cbv[t], pv_ref, tv_ref, t, tail_v[t], last)
                hs.append((_gelu_tanh(yg) * yv).astype(BF16))
            hcat = jnp.concatenate(hs, axis=1) if tiles > 1 else hs[0]
            acc_s[sl, :] += jnp.dot(hcat, wd_ref[0:tiles * tn, :], preferred_element_type=F32)

    if n_ff_tiles % n_tiles == 0:
        body(n_tiles)
    else:
        @pl.when(j < nj - 1)
        def _():
            body(n_tiles)

        @pl.when(j == nj - 1)
        def _():
            body(n_ff_tiles % n_tiles)

    @pl.when(j == nj - 1)
    def _():
        y_ref[...] = x_ref[...] + _rms(acc_s[...], gpost_ref[...])


def _ffn(x, g_pre, w_up, conv_w, conv_b, w_down, layer, g_post, prev, rows, sub_rows, tn, n_tiles):
    n, d = x.shape
    d_ff = w_up.shape[2] // 2
    nff = d_ff // tn
    nj = -(-nff // n_tiles)
    has_prev = prev is not None
    assert rows % sub_rows == 0 and (not has_prev or (sub_rows == rows and n_tiles == 1))
    assert w_down.shape[1] == nj * n_tiles * tn
    xmap = lambda i, j: (i, 0)
    fix = lambda i, j: (0, 0)

    def tile(t, off):
        return lambda i, j: jnp.minimum(j * n_tiles + t, nff - 1) + off

    def specs(shape, off, lead=None):
        if lead is None:
            return [pl.BlockSpec(shape, lambda i, j, f=tile(t, off): (0, f(i, j)))
                    for t in range(n_tiles)]
        return [pl.BlockSpec((None,) + shape, lambda i, j, f=tile(t, off): (lead, 0, f(i, j)))
                for t in range(n_tiles)]

    in_specs = ([pl.BlockSpec((rows, d), xmap), pl.BlockSpec((1, d), fix)]
                + specs((d, tn), 0, layer) + specs((d, tn), nff, layer)
                + specs((CONV_W, tn), 0) + specs((CONV_W, tn), nff)
                + specs((1, tn), 0) + specs((1, tn), nff)
                + [pl.BlockSpec((None, n_tiles * tn, d), lambda i, j: (layer, j, 0)),
                   pl.BlockSpec((1, d), fix)])
    args = ([x, g_pre] + [w_up] * (2 * n_tiles) + [conv_w] * (2 * n_tiles)
            + [conv_b] * (2 * n_tiles) + [w_down, g_post])
    if has_prev:
        in_specs += [pl.BlockSpec((rows, tn), lambda i, j: (i, j)),
                     pl.BlockSpec((rows, tn), lambda i, j: (i, j + nff))]
        args += [prev, prev]
        t_rows = rows
    else:
        t_rows = ROW_GROUP
    t_spec = pl.BlockSpec((t_rows, n_tiles * tn), lambda i, j: (i, j))
    nb = n // rows
    return pl.pallas_call(
        functools.partial(_ffn_kernel, rows=rows, sub_rows=sub_rows, has_prev=has_prev,
                          n_tiles=n_tiles, n_ff_tiles=nff),
        grid=(nb, nj),
        in_specs=in_specs,
        out_specs=[pl.BlockSpec((rows, d), xmap), t_spec, t_spec],
        out_shape=[jax.ShapeDtypeStruct((n, d), F32),
                   jax.ShapeDtypeStruct((nb * t_rows, nj * n_tiles * tn), F32),
                   jax.ShapeDtypeStruct((nb * t_rows, nj * n_tiles * tn), F32)],
        scratch_shapes=[pltpu.VMEM((rows, d), BF16), pltpu.VMEM((rows, d), F32)],
        compiler_params=_cparams(("parallel", "arbitrary")),
        name="ffn",
    )(*args)


def _pick(n, prefs):
    for p in prefs:
        if n % p == 0:
            return p
    return n


def kernel(x_prompt, x_sample, cache_k, cache_v, page_table, state_hgrn, state_conv, norm_mix_pre, norm_mix_post, norm_ffn_pre, norm_ffn_post, w_in, hgrn_lb, hgrn_norm, lam_q1, lam_k1, lam_q2, lam_k2, diff_norm, gmlp_ln_g, gmlp_ln_b, gmlp_ws, gmlp_bs, w_out, w_up, conv_w, conv_b, w_down):
    bp, tp, d = x_prompt.shape
    bs, ts, _ = x_sample.shape
    depth = w_in.shape[0]
    d_ff = w_down.shape[1]
    n_pool, page = cache_k.shape[1], cache_k.shape[2]
    past = page_table.shape[1] * page
    assert ts <= ROW_GROUP - (CONV_W - 1) and ts <= CHUNK_C
    assert tp % CHUNK_C == 0 and (bs * ROW_GROUP) % CHUNK_C == 0

    lb_soft = jax.nn.softmax(hgrn_lb.astype(F32), axis=0)
    lb_all = jnp.clip(jnp.cumsum(lb_soft, axis=0) - lb_soft[0], 0.0, 1.0 - 1e-6)

    tabs_p = _rope_tables(jnp.arange(tp))
    pos_s = jnp.minimum(jnp.arange(ROW_GROUP), ts - 1) + past
    tabs_s = tuple(jnp.tile(a, (bs, 1)) for a in _rope_tables(pos_s))

    ck = jnp.transpose(cache_k, (0, 1, 3, 4, 2)).reshape(depth, n_pool, D_QK, page)
    cv = cache_v.reshape(depth, n_pool, page * H_B, DV_B)
    n_grp = _pick(page_table.shape[1], (32, 16, 8, 4, 2))

    ns = bs * ROW_GROUP
    xp = x_prompt.reshape(bp * tp, d)
    xs = jnp.pad(x_sample, ((0, 0), (0, ROW_GROUP - ts), (0, 0))).reshape(ns, d)

    tm_p = _pick(tp, (512, 256, 128))
    tq = _pick(tp, (512, 256, 128))
    tn = _pick(d_ff, (256, 128))
    tn_s = _pick(d_ff, (1408, 256, 128))
    lc_p = _pick(tp, (CHUNK_A,))
    bs_p = _pick(lc_p, (16, ROW_GROUP))
    sub_p = _pick(tp, (256, 128))

    eye_g = jnp.eye(CHUNK_C // ROW_GROUP, dtype=F32)

    w_in_b, w_out_b, w_up_b, w_down_b = (w.astype(BF16) for w in (w_in, w_out, w_up, w_down))
    ffn_tiles = 4
    pad_ff = -d_ff % (ffn_tiles * tn)
    w_down_p = jnp.pad(w_down_b, ((0, 0), (0, pad_ff), (0, 0)))

    outs = {k: [] for k in ("k_p", "v_p", "k_s", "v_s", "hg_p", "hg_s", "gv_s", "cv_p", "cv_s")}
    for l in range(depth):
        lam_init = 0.8 - 0.6 * math.exp(-0.3 * l)
        lam = (jnp.exp(jnp.sum(lam_q1[l].astype(F32) * lam_k1[l].astype(F32)))
               - jnp.exp(jnp.sum(lam_q2[l].astype(F32) * lam_k2[l].astype(F32)))
               + lam_init).reshape(1, 1)
        lb = lb_all[l]
        lbp = jnp.stack([jnp.log(lb), jnp.log1p(-lb), 1.0 - lb])
        g_hgrn = jnp.tile(hgrn_norm[l], H_A)[None, :]
        g_diff = diff_norm[l][None, :]
        ws = gmlp_ws[l].astype(F32)
        bsl = gmlp_bs[l].astype(F32)
        bt_p = jnp.repeat(bsl.T, CG_C, axis=1)
        ws8 = jnp.pad(ws[:, :ts, :ts], ((0, 0), (0, ROW_GROUP - ts), (0, ROW_GROUP - ts)))
        ws_s = jnp.einsum('ab,gts->gatbs', eye_g, ws8).reshape(G_C, CHUNK_C, CHUNK_C)
        bt_s = jnp.tile(jnp.repeat(bsl.T[:ROW_GROUP], CG_C, axis=1), (CHUNK_C // ROW_GROUP, 1))
        lg = gmlp_ln_g[l][None, :]
        lbn = gmlp_ln_b[l][None, :]
        cb = conv_b[l][None, :]

        oa, qq, k, v, kb, vb, uv, st = _proj(xp, norm_mix_pre[l][None, :], w_in_b, l, tabs_p, tm_p,
                                             (lbp, g_hgrn, lc_p, bs_p))
        seq = lambda a: a.reshape(bp, tp, -1)
        ob = _attn_prompt(seq(qq), seq(kb), seq(vb), lam, g_diff, 1.0 - lam_init, tq)
        x1, _ = _outproj(oa.reshape(-1, D_A), ob.reshape(-1, D_B), uv, ws, bt_p, lg, lbn,
                         w_out_b, l, xp, norm_mix_post[l][None, :], tm_p, False)
        xp, tg, tv = _ffn(x1, norm_ffn_pre[l][None, :], w_up_b, conv_w[l], cb, w_down_p, l,
                          norm_ffn_post[l][None, :], None, tp, sub_p, tn, ffn_tiles)
        outs["k_p"].append(jnp.transpose(k.reshape(bp, 2 * H_B, DK_B, tp), (0, 3, 1, 2)))
        outs["v_p"].append(v.reshape(bp, tp, H_B, DV_B))
        outs["hg_p"].append(st)
        tail = jnp.concatenate([tg[:, :d_ff], tv[:, :d_ff]], axis=1).reshape(bp, ROW_GROUP, 2 * d_ff)
        outs["cv_p"].append(tail[:, ROW_GROUP - (CONV_W - 1):])

        hg, q, k, v, uv = _proj(xs, norm_mix_pre[l][None, :], w_in_b, l, tabs_s, ns)
        oa, st = _hgrn(hg.reshape(bs, ROW_GROUP, -1), lbp, g_hgrn, state_hgrn[l],
                       ROW_GROUP, ROW_GROUP, ROW_GROUP, _pick(bs, (8, 4, 2, 1)), ts)
        ob = _attn_decode(q, k, v, ck, cv, page_table, l, lam, g_diff, 1.0 - lam_init, ts, n_grp)
        x1, vn = _outproj(oa.reshape(-1, D_A), ob, uv, ws_s, bt_s, lg, lbn,
                          w_out_b, l, xs, norm_mix_post[l][None, :], ns, True)
        prev = jnp.pad(jnp.roll(state_conv[l].astype(F32), -1, axis=0),
                       ((0, 0), (ROW_GROUP - (CONV_W - 1), 0), (0, 0))).reshape(ns, 2 * d_ff)
        xs, tg, tv = _ffn(x1, norm_ffn_pre[l][None, :], w_up_b, conv_w[l], cb, w_down_b, l,
                          norm_ffn_post[l][None, :], prev, ns, ns, tn_s, 1)
        grp = lambda a: a.reshape(bs, ROW_GROUP, -1)[:, :ts]
        outs["k_s"].append(grp(k).reshape(bs, ts, 2 * H_B, DK_B))
        outs["v_s"].append(grp(v).reshape(bs, ts, H_B, DV_B))
        outs["hg_s"].append(st)
        outs["gv_s"].append(grp(vn))
        up_s = jnp.concatenate([tg, tv], axis=1).reshape(bs, ROW_GROUP, 2 * d_ff)
        outs["cv_s"].append(up_s[:, ts - (CONV_W - 1):ts])

    y_p = xp.reshape(bp, tp, d)
    y_s = xs.reshape(bs, ROW_GROUP, d)[:, :ts]
    st = lambda key: jnp.stack(outs[key])
    return (y_p, y_s, st("k_p"), st("v_p"), st("k_s"), st("v_s"), st("hg_p"), st("hg_s"),
            st("gv_s"), st("cv_p"), st("cv_s"))
```

```python
import functools
import math

import jax
import jax.numpy as jnp
from jax import lax
from jax.experimental import pallas as pl
from jax.experimental.pallas import tpu as pltpu

F32 = jnp.float32
BF16 = jnp.bfloat16

H_A = 4
DK_A = 64
D_A = H_A * DK_A
H_B = 4
DK_B = 64
DV_B = 128
D_QK = 2 * H_B * DK_B
D_B = H_B * DV_B
ROT_DIM = DK_B // 4
ROPE_THETA = 500000.0
G_C = 4
D_C = 256
CG_C = D_C // G_C
CHUNK_C = 128
CHUNK_A = 64
CONV_W = 3
ROW_GROUP = 8
LANES = 128
VMEM_LIMIT = 58 * 1024 * 1024
NEG_INF = float("-inf")


def _cparams(sem):
    return pltpu.CompilerParams(dimension_semantics=sem, vmem_limit_bytes=VMEM_LIMIT)


def _rms(x, g, eps=1e-6):
    return x * lax.rsqrt(jnp.mean(x * x, axis=-1, keepdims=True) + eps) * g


def _proj_kernel(x_ref, g_ref, w_ref, cos_ref, sa_ref, sb_ref, *refs, q_scale, attn_bf16,
                 nt, lc, bs):
    if attn_bf16:
        (lb_ref, gh_ref, oa_ref, q_ref, k_ref, v_ref, kb_ref, vb_ref, uv_ref, st_ref,
         st_s, cum_s, kk_s, v_s) = refs
    else:
        hg_ref, q_ref, k_ref, v_ref, uv_ref = refs
    h = _rms(x_ref[...], g_ref[...]).astype(BF16)

    def mm(lo, hi):
        return jnp.dot(h, w_ref[:, lo:hi], preferred_element_type=F32)

    n_hg = 4 * D_A
    if attn_bf16:
        step = pl.program_id(0)

        @pl.when(step % nt == 0)
        def _():
            st_s[...] = jnp.zeros(st_s.shape, F32)

        hg = mm(0, n_hg)
        st = st_s[...]
        for c in range(hg.shape[0] // lc):
            part = lambda k: hg[c * lc:(c + 1) * lc, k * D_A:(k + 1) * D_A]
            oa, st = _hgrn_chunk(part(0), part(1), part(2), part(3), st, lb_ref, gh_ref,
                                 cum_s.at[c], kk_s.at[c], v_s.at[c], lc=lc, bs=bs, t_valid=lc)
            oa_ref[c * lc:(c + 1) * lc, :] = oa
        st_s[...] = st

        @pl.when(step % nt == nt - 1)
        def _():
            st_ref[0] = _state_rows(st)
    else:
        hg_ref[...] = mm(0, n_hg)
    cos = cos_ref[...]
    sa = sa_ref[...]
    sb = sb_ref[...]
    half = ROT_DIM // 2

    def rope(xx):
        return (xx * cos + pltpu.roll(xx, LANES - half, 1) * sa
                + pltpu.roll(xx, half, 1) * sb)

    first = lax.broadcasted_iota(jnp.int32, (1, LANES), 1) < DK_B
    for j in range(D_QK // LANES):
        lo = n_hg + j * LANES
        qj = rope(mm(lo, lo + LANES)) * q_scale
        lo = n_hg + D_QK + j * LANES
        kj = rope(mm(lo, lo + LANES))
        if attn_bf16:
            k_ref[j * LANES:(j + 1) * LANES, :] = kj.T
            q_ref[:, 2 * j * LANES:(2 * j + 1) * LANES] = jnp.where(first, qj, 0.0).astype(BF16)
            q_ref[:, (2 * j + 1) * LANES:(2 * j + 2) * LANES] = jnp.where(first, 0.0, qj).astype(BF16)
            kb_ref[:, j * LANES:(j + 1) * LANES] = kj.astype(BF16)
        else:
            k_ref[:, j * LANES:(j + 1) * LANES] = kj
            q_ref[:, j * LANES:(j + 1) * LANES] = qj
    lo = n_hg + 2 * D_QK
    vv = mm(lo, lo + D_B)
    if attn_bf16:
        for hh in range(H_B):
            v_ref[:, hh, :] = vv[:, hh * DV_B:(hh + 1) * DV_B]
        vb_ref[...] = vv.astype(BF16)
    else:
        v_ref[...] = vv
    uv_ref[...] = mm(lo + D_B, lo + D_B + 2 * D_C)


def _proj(x, g, w_in, layer, tabs, tm, hgrn=None):
    n, d = x.shape
    n_in = w_in.shape[2]
    cos, sa, sb = tabs
    t_len = cos.shape[0]
    nt = t_len // tm
    row = lambda i: (i, 0)
    fix = lambda i: (0, 0)
    tab = lambda i: (i % nt, 0)
    fused = hgrn is not None
    in_specs = [pl.BlockSpec((tm, d), row), pl.BlockSpec((1, d), fix),
                pl.BlockSpec((None, d, n_in), lambda i: (layer, 0, 0)),
                pl.BlockSpec((tm, LANES), tab), pl.BlockSpec((tm, LANES), tab),
                pl.BlockSpec((tm, LANES), tab)]
    args = [x, g, w_in, cos, sa, sb]
    scratch = []
    lc = bs = 0
    if fused:
        lbp, g_hgrn, lc, bs = hgrn
        in_specs += [pl.BlockSpec((3, D_A), fix), pl.BlockSpec((1, D_A), fix)]
        args += [lbp, g_hgrn]
        outs = [(D_A, F32), (2 * D_QK, BF16), None, None, (D_QK, BF16), (D_B, BF16),
                (2 * D_C, F32), None]
    else:
        outs = [(4 * D_A, F32), (D_QK, F32), (D_QK, F32), (D_B, F32), (2 * D_C, F32)]
    out_specs = [o and pl.BlockSpec((tm, o[0]), row) for o in outs]
    out_shape = [o and jax.ShapeDtypeStruct((n, o[0]), o[1]) for o in outs]
    if fused:
        nb = n // t_len
        out_specs[2] = pl.BlockSpec((None, D_QK, tm), lambda i: (i // nt, 0, i % nt))
        out_shape[2] = jax.ShapeDtypeStruct((nb, D_QK, t_len), F32)
        out_specs[3] = pl.BlockSpec((tm, H_B, DV_B), lambda i: (i, 0, 0))
        out_shape[3] = jax.ShapeDtypeStruct((n, H_B, DV_B), F32)
        out_specs[7] = pl.BlockSpec((1, D_A, DK_A), lambda i: (i // nt, 0, 0))
        out_shape[7] = jax.ShapeDtypeStruct((nb, D_A, DK_A), F32)
        scratch = [pltpu.VMEM((D_A, D_A), F32)] + [pltpu.VMEM((tm // lc, lc, D_A), F32)] * 3
    res = pl.pallas_call(
        functools.partial(_proj_kernel, q_scale=DK_B ** -0.5, attn_bf16=fused, nt=nt, lc=lc, bs=bs),
        grid=(n // tm,),
        in_specs=in_specs,
        out_specs=out_specs,
        out_shape=out_shape,
        scratch_shapes=scratch,
        compiler_params=_cparams(("arbitrary",)),
        name="proj",
    )(*args)
    if fused:
        res = list(res)
        res[7] = res[7].reshape(n // t_len, H_A, DK_A, DK_A)
    return res


def _rope_tables(pos):
    half = ROT_DIM // 2
    inv = ROPE_THETA ** (-jnp.arange(half, dtype=F32) * (2.0 / ROT_DIM))
    ang = pos.astype(F32)[:, None] * inv[None, :]
    c, s = jnp.cos(ang), jnp.sin(ang)
    t = pos.shape[0]
    rest = DK_B - ROT_DIM
    cos64 = jnp.concatenate([c, c, jnp.ones((t, rest), F32)], axis=1)
    sa64 = jnp.concatenate([-s, jnp.zeros((t, half + rest), F32)], axis=1)
    sb64 = jnp.concatenate([jnp.zeros((t, half), F32), s, jnp.zeros((t, rest), F32)], axis=1)
    rep = LANES // DK_B
    return tuple(jnp.tile(a, (1, rep)) for a in (cos64, sa64, sb64))


def _hgrn_chunk(q, z, vi, ga, st, lb_ref, g_ref, cum_s, kk_s, v_s, *, lc, bs, t_valid):
    nblk = lc // bs
    log_sig = jnp.minimum(z, 0.0) - jnp.log1p(jnp.exp(-jnp.abs(z)))
    a = lb_ref[0:1, :]
    b = lb_ref[1:2, :] + log_sig
    log_f = jnp.maximum(a, b) + jnp.log1p(jnp.exp(-jnp.abs(a - b)))
    kk = lb_ref[2:3, :] * jax.nn.sigmoid(-z)
    rows = lax.broadcasted_iota(jnp.int32, (lc, 1), 0)
    if t_valid < lc:
        log_f = jnp.where(rows < t_valid, log_f, 0.0)
        kk = jnp.where(rows < t_valid, kk, 0.0)

    cum = log_f
    d = 1
    while d < lc:
        cum = cum + jnp.where(rows >= d, pltpu.roll(cum, d, 0), 0.0)
        d *= 2

    cum_s[...] = cum
    kk_s[...] = kk
    v_s[...] = vi

    hr = lax.broadcasted_iota(jnp.int32, (D_A, D_A), 0) // DK_A
    hc = lax.broadcasted_iota(jnp.int32, (D_A, D_A), 1) // DK_A
    same_head = hr == hc
    ones_bd = same_head.astype(BF16)

    q_dec = (q * jnp.exp(cum)).astype(BF16)
    o_inter = lax.dot_general(q_dec, st.astype(BF16), (((1,), (1,)), ((), ())),
                              preferred_element_type=F32)
    o_blk = [o_inter[i * bs:(i + 1) * bs] for i in range(nblk)]

    sub = ROW_GROUP
    for i in range(nblk):
        r0 = i * bs
        pieces = []
        for s in range(bs):
            lo = r0 + (s // sub) * sub
            cs = cum_s[r0 + s:r0 + s + 1, :]
            ks = kk_s[r0 + s:r0 + s + 1, :]
            dd = cum[lo:r0 + bs] - cs
            if s % sub:
                dd = jnp.where(rows[lo:r0 + bs] >= r0 + s, dd, NEG_INF)
            pieces.append(q[lo:r0 + bs] * jnp.exp(dd) * ks)
        p_all = jnp.dot(jnp.concatenate(pieces, axis=0).astype(BF16), ones_bd,
                        preferred_element_type=F32)
        off = 0
        acc = {}
        for s in range(bs):
            lo = (s // sub) * sub
            n = bs - lo
            contrib = p_all[off:off + n] * v_s[r0 + s:r0 + s + 1, :]
            acc[lo] = contrib if lo not in acc else acc[lo] + contrib
            off += n
        tot = acc[0]
        for lo, val in acc.items():
            if lo:
                tot = tot + jnp.concatenate([jnp.zeros((lo, D_A), F32), val], axis=0)
        o_blk[i] = o_blk[i] + tot

    if nblk > 1:
        hm = (lax.broadcasted_iota(jnp.int32, (H_A * bs, D_A), 0) // bs
              == lax.broadcasted_iota(jnp.int32, (H_A * bs, D_A), 1) // DK_A)
        for j in range(nblk - 1):
            r1 = (j + 1) * bs
            aj = cum[r1 - 1:r1, :]
            k_t = kk[r1 - bs:r1] * jnp.exp(aj - cum[r1 - bs:r1])
            q_t = (q[r1:] * jnp.exp(cum[r1:] - aj)).astype(BF16)
            k_bd = jnp.where(hm, jnp.concatenate([k_t] * H_A, axis=0), 0.0).astype(BF16)
            v_bd = jnp.where(hm, jnp.concatenate([vi[r1 - bs:r1]] * H_A, axis=0), 0.0).astype(BF16)
            s2 = lax.dot_general(q_t, k_bd, (((1,), (1,)), ((), ())),
                                 preferred_element_type=F32)
            o_off = jnp.dot(s2.astype(BF16), v_bd, preferred_element_type=F32)
            for i in range(j + 1, nblk):
                o_blk[i] = o_blk[i] + o_off[(i - j - 1) * bs:(i - j) * bs]

    last = cum[lc - 1:lc, :]
    k_dec = (kk * jnp.exp(last - cum)).astype(BF16)
    upd = lax.dot_general(vi.astype(BF16), k_dec, (((0,), (0,)), ((), ())),
                          preferred_element_type=F32)
    st_new = st * jnp.exp(last) + jnp.where(same_head, upd, 0.0)

    o = jnp.concatenate(o_blk, axis=0) if nblk > 1 else o_blk[0]
    sq = o * o
    hi = sq.astype(BF16)
    lo_ = (sq - hi.astype(F32)).astype(BF16)
    ms = (jnp.dot(hi, ones_bd, preferred_element_type=F32)
          + jnp.dot(lo_, ones_bd, preferred_element_type=F32)) * (1.0 / DK_A)
    oa = o * lax.rsqrt(ms + 1e-6) * g_ref[...] * (ga * jax.nn.sigmoid(ga))
    return oa, st_new


def _state_rows(st):
    bd = st.T
    tot = bd[:, 0:DK_A]
    for h in range(1, H_A):
        tot = tot + bd[:, h * DK_A:(h + 1) * DK_A]
    return tot


def _hgrn_kernel(hg_ref, lb_ref, g_ref, st0_ref, oa_ref, st_ref, st_s, cum_s, kk_s, v_s,
                 *, lc, bs, n_chunks, n_seq, t_valid):
    t_idx = pl.program_id(1)

    @pl.when(t_idx == 0)
    def _():
        same_head = (lax.broadcasted_iota(jnp.int32, (D_A, D_A), 0) // DK_A
                     == lax.broadcasted_iota(jnp.int32, (D_A, D_A), 1) // DK_A)
        for i in range(n_seq):
            x4 = jnp.concatenate([st0_ref[i]] * H_A, axis=1)
            st_s[i] = jnp.where(same_head, x4, 0.0).T

    for i in range(n_seq):
        for ci in range(n_chunks):
            rows = pl.ds(ci * lc, lc)
            blk = lambda k: hg_ref[i, rows, k * D_A:(k + 1) * D_A]
            oa, st_new = _hgrn_chunk(blk(0), blk(1), blk(2), blk(3), st_s[i], lb_ref, g_ref,
                                     cum_s.at[i], kk_s.at[i], v_s.at[i],
                                     lc=lc, bs=bs, t_valid=t_valid)
            oa_ref[i, rows, :] = oa
            st_s[i] = st_new

    @pl.when(t_idx == pl.num_programs(1) - 1)
    def _():
        for i in range(n_seq):
            st_ref[i] = _state_rows(st_s[i])


def _hgrn(hg, lbp, g, st0, lc, bs, tb, n_seq, t_valid):
    b, t, w = hg.shape
    smap = lambda i, c: (i, 0, 0)
    in_specs = [pl.BlockSpec((n_seq, tb, w), lambda i, c: (i, c, 0)),
                pl.BlockSpec((3, D_A), lambda i, c: (0, 0)),
                pl.BlockSpec((1, D_A), lambda i, c: (0, 0)),
                pl.BlockSpec((n_seq, D_A, DK_A), smap)]
    args = [hg, lbp, g, st0.astype(F32).reshape(b, D_A, DK_A)]
    oa, st = pl.pallas_call(
        functools.partial(_hgrn_kernel, lc=lc, bs=bs, n_chunks=tb // lc, n_seq=n_seq,
                          t_valid=t_valid),
        grid=(b // n_seq, t // tb),
        in_specs=in_specs,
        out_specs=[pl.BlockSpec((n_seq, tb, D_A), lambda i, c: (i, c, 0)),
                   pl.BlockSpec((n_seq, D_A, DK_A), smap)],
        out_shape=[jax.ShapeDtypeStruct((b, t, D_A), F32),
                   jax.ShapeDtypeStruct((b, D_A, DK_A), F32)],
        scratch_shapes=([pltpu.VMEM((n_seq, D_A, D_A), F32)]
                        + [pltpu.VMEM((n_seq, lc, D_A), F32)] * 3),
        compiler_params=_cparams(("parallel", "arbitrary")),
        name="hgrn",
    )(*args)
    return oa, st.reshape(b, H_A, DK_A, DK_A)


def _attn_kernel(lam_ref, q1_ref, q2_ref, k_ref, v_ref, g_ref, o_ref, *, tq, out_scale):
    t = k_ref.shape[1]
    nq = t // tq
    keep = (lax.broadcasted_iota(jnp.int32, (tq, tq), 0)
            <= lax.broadcasted_iota(jnp.int32, (tq, tq), 1))
    lam = lam_ref[...]
    for qi in range(nq):
        qsl = pl.ds(qi * tq, tq)
        heads = []
        for q_ref in (q1_ref, q2_ref):
            q = q_ref[0, qsl, :]
            m = jnp.full((1, tq), NEG_INF, F32)
            l = jnp.zeros((1, tq), F32)
            acc = jnp.zeros((DV_B, tq), F32)
            for ki in range(qi + 1):
                ksl = pl.ds(ki * tq, tq)
                st = lax.dot_general(k_ref[0, ksl, :], q, (((1,), (1,)), ((), ())),
                                     preferred_element_type=F32)
                if ki == qi:
                    st = jnp.where(keep, st, NEG_INF)
                m_new = jnp.maximum(m, jnp.max(st, axis=0, keepdims=True))
                alpha = jnp.exp(m - m_new)
                p = jnp.exp(st - m_new)
                l = alpha * l + jnp.sum(p, axis=0, keepdims=True)
                pv = lax.dot_general(v_ref[0, ksl, :], p.astype(BF16), (((0,), (0,)), ((), ())),
                                     preferred_element_type=F32)
                acc = alpha * acc + pv
                m = m_new
            heads.append(acc * (1.0 / l))
        o = heads[0] - lam * heads[1]
        o = o * lax.rsqrt(jnp.mean(o * o, axis=0, keepdims=True) + 1e-6)
        o_ref[0, qsl, :] = o.T * g_ref[...] * out_scale


def _attn_prompt(qq, kb, vb, lam, g, out_scale, tq):
    b, t, _ = kb.shape
    fix = lambda i, j: (0, 0)

    def qmap(sub):
        return lambda i, j: (i, 0, 2 * j + sub)

    kmap = lambda i, j: (i, 0, j)
    return pl.pallas_call(
        functools.partial(_attn_kernel, tq=tq, out_scale=out_scale),
        grid=(b, H_B),
        in_specs=[pl.BlockSpec((1, 1), fix),
                  pl.BlockSpec((1, t, LANES), qmap(0)),
                  pl.BlockSpec((1, t, LANES), qmap(1)),
                  pl.BlockSpec((1, t, LANES), kmap),
                  pl.BlockSpec((1, t, LANES), kmap),
                  pl.BlockSpec((1, DV_B), fix)],
        out_specs=pl.BlockSpec((1, t, LANES), kmap),
        out_shape=jax.ShapeDtypeStruct((b, t, D_B), F32),
        compiler_params=_cparams(("parallel", "parallel")),
        name="attn_prompt",
    )(lam, qq, qq, kb, vb, g)


def _decode_kernel(pt_ref, lam_ref, q_ref, kn_ref, vn_ref, *rest, n_grp, t_valid, out_scale):
    k_refs = rest[:n_grp]
    v_refs = rest[n_grp:2 * n_grp]
    g_ref, o_ref, qbd_s, m_s, l_s, acc_s = rest[2 * n_grp:]
    p_idx = pl.program_id(1)
    n_sub = 2 * H_B
    nr = n_sub * ROW_GROUP
    page = k_refs[0].shape[1]
    hrow = lax.broadcasted_iota(jnp.int32, (nr, 1), 0) // ROW_GROUP
    hcol = lax.broadcasted_iota(jnp.int32, (1, D_QK), 1) // DK_B

    @pl.when(p_idx == 0)
    def _():
        qt = jnp.concatenate([q_ref[...]] * n_sub, axis=0)
        qbd_s[...] = jnp.where(hrow == hcol, qt, 0.0).astype(BF16)
        m_s[...] = jnp.full(m_s.shape, NEG_INF, F32)
        l_s[...] = jnp.zeros(l_s.shape, F32)
        acc_s[...] = jnp.zeros(acc_s.shape, F32)

    def update(s, pv):
        m_prev = m_s[...]
        m_new = jnp.maximum(m_prev, jnp.max(s, axis=-1, keepdims=True))
        alpha = jnp.exp(m_prev - m_new)
        p = jnp.exp(s - m_new)
        l_s[...] = alpha * l_s[...] + jnp.sum(p, axis=-1, keepdims=True)
        acc_s[...] = alpha * acc_s[...] + pv(p.astype(BF16))
        m_s[...] = m_new

    qbd = qbd_s[...]
    s_pages = [jnp.dot(qbd, k_refs[i][...].astype(BF16), preferred_element_type=F32)
               for i in range(n_grp)]

    def pv_pages(p):
        tot = None
        for i in range(n_grp):
            vcat = jnp.concatenate(
                [v_refs[i][pl.ds(h, page, stride=H_B), :] for h in range(H_B)], axis=1)
            t = jnp.dot(p[:, i * page:(i + 1) * page], vcat.astype(BF16),
                        preferred_element_type=F32)
            tot = t if tot is None else tot + t
        return tot

    update(jnp.concatenate(s_pages, axis=1) if n_grp > 1 else s_pages[0], pv_pages)

    @pl.when(p_idx == pl.num_programs(1) - 1)
    def _():
        zpad = jnp.zeros((page - ROW_GROUP, D_QK), F32)
        kn = jnp.concatenate([kn_ref[...], zpad], axis=0).astype(BF16)
        vn = jnp.concatenate([vn_ref[...], zpad], axis=0).astype(BF16)
        r = lax.broadcasted_iota(jnp.int32, (nr, page), 0) % ROW_GROUP
        c = lax.broadcasted_iota(jnp.int32, (nr, page), 1)
        s_new = lax.dot_general(qbd, kn, (((1,), (1,)), ((), ())), preferred_element_type=F32)
        s_new = jnp.where((c <= r) & (c < t_valid), s_new, NEG_INF)
        update(s_new, lambda p: jnp.dot(p, vn, preferred_element_type=F32))
        lam = lam_ref[...]
        coef = jnp.where(hrow % 2 == 0, 1.0, -lam)
        vcol = lax.broadcasted_iota(jnp.int32, (1, D_B), 1) // DV_B
        contrib = jnp.where(vcol == hrow // 2, acc_s[...] / l_s[...] * coef, 0.0)
        o = contrib[0:ROW_GROUP]
        for h in range(1, n_sub):
            o = o + contrib[h * ROW_GROUP:(h + 1) * ROW_GROUP]
        for j in range(H_B):
            oj = o[:, j * DV_B:(j + 1) * DV_B]
            o_ref[:, j * DV_B:(j + 1) * DV_B] = _rms(oj, g_ref[...]) * out_scale


def _attn_decode(q, k_new, v_new, cache_kt, cache_v2, page_table, layer, lam, g, out_scale,
                 t_valid, n_grp):
    n = q.shape[0]
    nb, n_pages = page_table.shape
    page = cache_kt.shape[3]
    row = lambda i, p, pt: (i, 0)
    fix = lambda i, p, pt: (0, 0)

    def cmap(gi):
        return lambda i, p, pt: (layer, pt[i, p * n_grp + gi], 0, 0)

    nr = 2 * H_B * ROW_GROUP
    grid_spec = pltpu.PrefetchScalarGridSpec(
        num_scalar_prefetch=1,
        grid=(nb, n_pages // n_grp),
        in_specs=([pl.BlockSpec((1, 1), fix),
                   pl.BlockSpec((ROW_GROUP, D_QK), row),
                   pl.BlockSpec((ROW_GROUP, D_QK), row),
                   pl.BlockSpec((ROW_GROUP, D_B), row)]
                  + [pl.BlockSpec((None, None, D_QK, page), cmap(gi)) for gi in range(n_grp)]
                  + [pl.BlockSpec((None, None, page * H_B, DV_B), cmap(gi)) for gi in range(n_grp)]
                  + [pl.BlockSpec((1, DV_B), fix)]),
        out_specs=pl.BlockSpec((ROW_GROUP, D_B), row),
        scratch_shapes=[pltpu.VMEM((nr, D_QK), BF16), pltpu.VMEM((nr, 1), F32),
                        pltpu.VMEM((nr, 1), F32), pltpu.VMEM((nr, D_B), F32)],
    )
    return pl.pallas_call(
        functools.partial(_decode_kernel, n_grp=n_grp, t_valid=t_valid, out_scale=out_scale),
        grid_spec=grid_spec,
        out_shape=jax.ShapeDtypeStruct((n, D_B), F32),
        compiler_params=_cparams(("parallel", "arbitrary")),
        name="attn_decode",
    )(page_table, lam, q, k_new, v_new, *([cache_kt] * n_grp), *([cache_v2] * n_grp), g)


def _outproj_kernel(oa_ref, ob_ref, uv_ref, ws_ref, bt_ref, lg_ref, lb_ref, w_ref, x_ref, g_ref,
                    o_ref, *vn_out, n_chunks):
    tr = lax.broadcasted_iota(jnp.int32, (CHUNK_C, CHUNK_C), 0)
    tc = lax.broadcasted_iota(jnp.int32, (CHUNK_C, CHUNK_C), 1)
    causal = tc <= tr
    grp = lax.broadcasted_iota(jnp.int32, (1, D_C), 1) // CG_C
    wgs = [jnp.where(causal, ws_ref[gi], 0.0).astype(BF16) for gi in range(G_C)]
    ocs = []
    for c in range(n_chunks):
        sl = pl.ds(c * CHUNK_C, CHUNK_C)
        u = uv_ref[sl, 0:D_C]
        v = uv_ref[sl, D_C:2 * D_C]
        mu = jnp.mean(v, axis=-1, keepdims=True)
        vc = v - mu
        var = jnp.mean(vc * vc, axis=-1, keepdims=True)
        vn = vc * lax.rsqrt(var + 1e-5) * lg_ref[...] + lb_ref[...]
        if vn_out:
            vn_out[0][sl, :] = vn
        vnb = vn.astype(BF16)
        mixed = bt_ref[...]
        for gi in range(G_C):
            mg = jnp.dot(wgs[gi], vnb, preferred_element_type=F32)
            mixed = mixed + jnp.where(grp == gi, mg, 0.0)
        ocs.append((u * mixed).astype(BF16))
    oc = jnp.concatenate(ocs, axis=0) if n_chunks > 1 else ocs[0]
    mix = jnp.dot(oa_ref[...].astype(BF16), w_ref[0:D_A, :], preferred_element_type=F32)
    mix += jnp.dot(ob_ref[...].astype(BF16), w_ref[D_A:D_A + D_B, :],
                   preferred_element_type=F32)
    mix += jnp.dot(oc, w_ref[D_A + D_B:, :], preferred_element_type=F32)
    o_ref[...] = x_ref[...] + _rms(mix, g_ref[...])


def _outproj(oa, ob, uv, ws, bt, lg, lb, w_out, layer, x, g, tm, want_vn):
    n, d = x.shape
    row = lambda i: (i, 0)
    fix = lambda i: (0, 0)
    out_specs = [pl.BlockSpec((tm, d), row)]
    out_shape = [jax.ShapeDtypeStruct((n, d), F32)]
    if want_vn:
        out_specs.append(pl.BlockSpec((tm, D_C), row))
        out_shape.append(jax.ShapeDtypeStruct((n, D_C), F32))
    res = pl.pallas_call(
        functools.partial(_outproj_kernel, n_chunks=tm // CHUNK_C),
        grid=(n // tm,),
        in_specs=[pl.BlockSpec((tm, D_A), row), pl.BlockSpec((tm, D_B), row),
                  pl.BlockSpec((tm, 2 * D_C), row),
                  pl.BlockSpec((G_C, CHUNK_C, CHUNK_C), lambda i: (0, 0, 0)),
                  pl.BlockSpec((CHUNK_C, D_C), fix),
                  pl.BlockSpec((1, D_C), fix), pl.BlockSpec((1, D_C), fix),
                  pl.BlockSpec((None,) + w_out.shape[1:], lambda i: (layer, 0, 0)),
                  pl.BlockSpec((tm, d), row), pl.BlockSpec((1, d), fix)],
        out_specs=out_specs,
        out_shape=out_shape,
        compiler_params=_cparams(("parallel",)),
        name="outproj",
    )(oa, ob, uv, ws, bt, lg, lb, w_out, x, g)
    return res if want_vn else (res[0], None)


def _gelu_tanh(x):
    c = math.sqrt(2.0 / math.pi)
    return (0.5 * x) * (1.0 + jnp.tanh(x * (c + (c * 0.044715) * (x * x))))


def _ffn_kernel(*refs, rows, sub_rows, has_prev, n_tiles, n_ff_tiles):
    it = iter(refs)
    x_ref, gpre_ref = next(it), next(it)
    wg, wv, cwg, cwv, cbg, cbv = ([next(it) for _ in range(n_tiles)] for _ in range(6))
    wd_ref, gpost_ref = next(it), next(it)
    pg_ref, pv_ref = (next(it), next(it)) if has_prev else (None, None)
    y_ref, tg_ref, tv_ref, hn_s, acc_s = it
    j = pl.program_id(1)
    nj = pl.num_programs(1)

    @pl.when(j == 0)
    def _():
        hn_s[...] = _rms(x_ref[...], gpre_ref[...]).astype(BF16)
        acc_s[...] = jnp.zeros(acc_s.shape, F32)

    tn = wg[0].shape[1]
    n_sub = rows // sub_rows
    g8 = ROW_GROUP

    def branch(hn, w_ref, cw_ref, cb_ref, p_ref, t_ref, col, tail, last):
        up = jnp.dot(hn, w_ref[...], preferred_element_type=F32)
        if has_prev:
            ridx = lax.broadcasted_iota(jnp.int32, (rows, 1), 0)
            up = jnp.where(ridx % g8 >= g8 - (CONV_W - 1), p_ref[...], up)
            t_ref[...] = up
            sh1 = pltpu.roll(up, 1, 0)
            sh2 = pltpu.roll(up, 2, 0)
        else:
            if last:
                t_ref[:, col * tn:(col + 1) * tn] = up[sub_rows - g8:, :]
            head = jnp.concatenate([tail, up[0:g8]], axis=0)
            sh1 = jnp.concatenate([pltpu.roll(head, 1, 0)[g8:], pltpu.roll(up, 1, 0)[g8:]], axis=0)
            sh2 = jnp.concatenate([pltpu.roll(head, 2, 0)[g8:], pltpu.roll(up, 2, 0)[g8:]], axis=0)
        y = cb_ref[...] + cw_ref[0:1, :] * sh2 + cw_ref[1:2, :] * sh1 + cw_ref[2:3, :] * up
        return y, up[sub_rows - g8:, :]

    def body(tiles):
        tail_g = [jnp.zeros((g8, tn), F32)] * tiles
        tail_v = list(tail_g)
        for t in range(tiles, n_tiles):
            tg_ref[:, t * tn:(t + 1) * tn] = jnp.zeros((tg_ref.shape[0], tn), F32)
            tv_ref[:, t * tn:(t + 1) * tn] = jnp.zeros((tv_ref.shape[0], tn), F32)
        for r in range(n_sub):
            sl = pl.ds(r * sub_rows, sub_rows)
            hn = hn_s[sl, :]
            last = r == n_sub - 1
            hs = []
            for t in range(tiles):
                yg, tail_g[t] = branch(hn, wg[t], cwg[t], cbg[t], pg_ref, tg_ref, t, tail_g[t], last)
                yv, tail_v[t] = branch(hn, wv[t], cwv[t], cbv[t], pv_ref, tv_ref, t, tail_v[t], last)
                hs.append((_gelu_tanh(yg) * yv).astype(BF16))
            hcat = jnp.concatenate(hs, axis=1) if tiles > 1 else hs[0]
            acc_s[sl, :] += jnp.dot(hcat, wd_ref[0:tiles * tn, :], preferred_element_type=F32)

    if n_ff_tiles % n_tiles == 0:
        body(n_tiles)
    else:
        @pl.when(j < nj - 1)
        def _():
            body(n_tiles)

        @pl.when(j == nj - 1)
        def _():
            body(n_ff_tiles % n_tiles)

    @pl.when(j == nj - 1)
    def _():
        y_ref[...] = x_ref[...] + _rms(acc_s[...], gpost_ref[...])


def _ffn(x, g_pre, w_up, conv_w, conv_b, w_down, layer, g_post, prev, rows, sub_rows, tn, n_tiles):
    n, d = x.shape
    d_ff = w_up.shape[2] // 2
    nff = d_ff // tn
    nj = -(-nff // n_tiles)
    has_prev = prev is not None
    assert rows % sub_rows == 0 and (not has_prev or (sub_rows == rows and n_tiles == 1))
    assert w_down.shape[1] == nj * n_tiles * tn
    xmap = lambda i, j: (i, 0)
    fix = lambda i, j: (0, 0)

    def tile(t, off):
        return lambda i, j: jnp.minimum(j * n_tiles + t, nff - 1) + off

    def specs(shape, off, lead=None):
        if lead is None:
            return [pl.BlockSpec(shape, lambda i, j, f=tile(t, off): (0, f(i, j)))
                    for t in range(n_tiles)]
        return [pl.BlockSpec((None,) + shape, lambda i, j, f=tile(t, off): (lead, 0, f(i, j)))
                for t in range(n_tiles)]

    in_specs = ([pl.BlockSpec((rows, d), xmap), pl.BlockSpec((1, d), fix)]
                + specs((d, tn), 0, layer) + specs((d, tn), nff, layer)
                + specs((CONV_W, tn), 0) + specs((CONV_W, tn), nff)
                + specs((1, tn), 0) + specs((1, tn), nff)
                + [pl.BlockSpec((None, n_tiles * tn, d), lambda i, j: (layer, j, 0)),
                   pl.BlockSpec((1, d), fix)])
    args = ([x, g_pre] + [w_up] * (2 * n_tiles) + [conv_w] * (2 * n_tiles)
            + [conv_b] * (2 * n_tiles) + [w_down, g_post])
    if has_prev:
        in_specs += [pl.BlockSpec((rows, tn), lambda i, j: (i, j)),
                     pl.BlockSpec((rows, tn), lambda i, j: (i, j + nff))]
        args += [prev, prev]
        t_rows = rows
    else:
        t_rows = ROW_GROUP
    t_spec = pl.BlockSpec((t_rows, n_tiles * tn), lambda i, j: (i, j))
    nb = n // rows
    return pl.pallas_call(
        functools.partial(_ffn_kernel, rows=rows, sub_rows=sub_rows, has_prev=has_prev,
                          n_tiles=n_tiles, n_ff_tiles=nff),
        grid=(nb, nj),
        in_specs=in_specs,
        out_specs=[pl.BlockSpec((rows, d), xmap), t_spec, t_spec],
        out_shape=[jax.ShapeDtypeStruct((n, d), F32),
                   jax.ShapeDtypeStruct((nb * t_rows, nj * n_tiles * tn), F32),
                   jax.ShapeDtypeStruct((nb * t_rows, nj * n_tiles * tn), F32)],
        scratch_shapes=[pltpu.VMEM((rows, d), BF16), pltpu.VMEM((rows, d), F32)],
        compiler_params=_cparams(("parallel", "arbitrary")),
        name="ffn",
    )(*args)


def _pick(n, prefs):
    for p in prefs:
        if n % p == 0:
            return p
    return n


def kernel(x_prompt, x_sample, cache_k, cache_v, page_table, state_hgrn, state_conv, norm_mix_pre, norm_mix_post, norm_ffn_pre, norm_ffn_post, w_in, hgrn_lb, hgrn_norm, lam_q1, lam_k1, lam_q2, lam_k2, diff_norm, gmlp_ln_g, gmlp_ln_b, gmlp_ws, gmlp_bs, w_out, w_up, conv_w, conv_b, w_down):
    bp, tp, d = x_prompt.shape
    bs, ts, _ = x_sample.shape
    depth = w_in.shape[0]
    d_ff = w_down.shape[1]
    n_pool, page = cache_k.shape[1], cache_k.shape[2]
    past = page_table.shape[1] * page
    assert ts <= ROW_GROUP - (CONV_W - 1) and ts <= CHUNK_C
    assert tp % CHUNK_C == 0 and (bs * ROW_GROUP) % CHUNK_C == 0

    lb_soft = jax.nn.softmax(hgrn_lb.astype(F32), axis=0)
    lb_all = jnp.clip(jnp.cumsum(lb_soft, axis=0) - lb_soft[0], 0.0, 1.0 - 1e-6)

    tabs_p = _rope_tables(jnp.arange(tp))
    pos_s = jnp.minimum(jnp.arange(ROW_GROUP), ts - 1) + past
    tabs_s = tuple(jnp.tile(a, (bs, 1)) for a in _rope_tables(pos_s))

    ck = jnp.transpose(cache_k, (0, 1, 3, 4, 2)).reshape(depth, n_pool, D_QK, page)
    cv = cache_v.reshape(depth, n_pool, page * H_B, DV_B)
    n_grp = _pick(page_table.shape[1], (32, 16, 8, 4, 2))

    ns = bs * ROW_GROUP
    xp = x_prompt.reshape(bp * tp, d)
    xs = jnp.pad(x_sample, ((0, 0), (0, ROW_GROUP - ts), (0, 0))).reshape(ns, d)

    tm_p = _pick(tp, (1024, 512, 256, 128))
    tq = _pick(tp, (512, 256, 128))
    tn = _pick(d_ff, (256, 128))
    tn_s = _pick(d_ff, (1408, 256, 128))
    lc_p = _pick(tp, (CHUNK_A,))
    bs_p = _pick(lc_p, (16, ROW_GROUP))
    sub_p = _pick(tp, (256, 128))

    eye_g = jnp.eye(CHUNK_C // ROW_GROUP, dtype=F32)

    w_in_b, w_out_b, w_up_b, w_down_b = (w.astype(BF16) for w in (w_in, w_out, w_up, w_down))
    ffn_tiles = 4
    pad_ff = -d_ff % (ffn_tiles * tn)
    w_down_p = jnp.pad(w_down_b, ((0, 0), (0, pad_ff), (0, 0)))

    outs = {k: [] for k in ("k_p", "v_p", "k_s", "v_s", "hg_p", "hg_s", "gv_s", "cv_p", "cv_s")}
    for l in range(depth):
        lam_init = 0.8 - 0.6 * math.exp(-0.3 * l)
        lam = (jnp.exp(jnp.sum(lam_q1[l].astype(F32) * lam_k1[l].astype(F32)))
               - jnp.exp(jnp.sum(lam_q2[l].astype(F32) * lam_k2[l].astype(F32)))
               + lam_init).reshape(1, 1)
        lb = lb_all[l]
        lbp = jnp.stack([jnp.log(lb), jnp.log1p(-lb), 1.0 - lb])
        g_hgrn = jnp.tile(hgrn_norm[l], H_A)[None, :]
        g_diff = diff_norm[l][None, :]
        ws = gmlp_ws[l].astype(F32)
        bsl = gmlp_bs[l].astype(F32)
        bt_p = jnp.repeat(bsl.T, CG_C, axis=1)
        ws8 = jnp.pad(ws[:, :ts, :ts], ((0, 0), (0, ROW_GROUP - ts), (0, ROW_GROUP - ts)))
        ws_s = jnp.einsum('ab,gts->gatbs', eye_g, ws8).reshape(G_C, CHUNK_C, CHUNK_C)
        bt_s = jnp.tile(jnp.repeat(bsl.T[:ROW_GROUP], CG_C, axis=1), (CHUNK_C // ROW_GROUP, 1))
        lg = gmlp_ln_g[l][None, :]
        lbn = gmlp_ln_b[l][None, :]
        cb = conv_b[l][None, :]

        oa, qq, k, v, kb, vb, uv, st = _proj(xp, norm_mix_pre[l][None, :], w_in_b, l, tabs_p, tm_p,
                                             (lbp, g_hgrn, lc_p, bs_p))
        seq = lambda a: a.reshape(bp, tp, -1)
        ob = _attn_prompt(seq(qq), seq(kb), seq(vb), lam, g_diff, 1.0 - lam_init, tq)
        x1, _ = _outproj(oa.reshape(-1, D_A), ob.reshape(-1, D_B), uv, ws, bt_p, lg, lbn,
                         w_out_b, l, xp, norm_mix_post[l][None, :], tm_p, False)
        xp, tg, tv = _ffn(x1, norm_ffn_pre[l][None, :], w_up_b, conv_w[l], cb, w_down_p, l,
                          norm_ffn_post[l][None, :], None, tp, sub_p, tn, ffn_tiles)
        outs["k_p"].append(jnp.transpose(k.reshape(bp, 2 * H_B, DK_B, tp), (0, 3, 1, 2)))
        outs["v_p"].append(v.reshape(bp, tp, H_B, DV_B))
        outs["hg_p"].append(st)
        tail = jnp.concatenate([tg[:, :d_ff], tv[:, :d_ff]], axis=1).reshape(bp, ROW_GROUP, 2 * d_ff)
        outs["cv_p"].append(tail[:, ROW_GROUP - (CONV_W - 1):])

        hg, q, k, v, uv = _proj(xs, norm_mix_pre[l][None, :], w_in_b, l, tabs_s, ns)
        oa, st = _hgrn(hg.reshape(bs, ROW_GROUP, -1), lbp, g_hgrn, state_hgrn[l],
                       ROW_GROUP, ROW_GROUP, ROW_GROUP, _pick(bs, (8, 4, 2, 1)), ts)
        ob = _attn_decode(q, k, v, ck, cv, page_table, l, lam, g_diff, 1.0 - lam_init, ts, n_grp)
        x1, vn = _outproj(oa.reshape(-1, D_A), ob, uv, ws_s, bt_s, lg, lbn,
                          w_out_b, l, xs, norm_mix_post[l][None, :], ns, True)
        prev = jnp.pad(jnp.roll(state_conv[l].astype(F32), -1, axis=0),
                       ((0, 0), (ROW_GROUP - (CONV_W - 1), 0), (0, 0))).reshape(ns, 2 * d_ff)
        xs, tg, tv = _ffn(x1, norm_ffn_pre[l][None, :], w_up_b, conv_w[l], cb, w_down_b, l,
                          norm_ffn_post[l][None, :], prev, ns, ns, tn_s, 1)
        grp = lambda a: a.reshape(bs, ROW_GROUP, -1)[:, :ts]
        outs["k_s"].append(grp(k).reshape(bs, ts, 2 * H_B, DK_B))
        outs["v_s"].append(grp(v).reshape(bs, ts, H_B, DV_B))
        outs["hg_s"].append(st)
        outs["gv_s"].append(grp(vn))
        up_s = jnp.concatenate([tg, tv], axis=1).reshape(bs, ROW_GROUP, 2 * d_ff)
        outs["cv_s"].append(up_s[:, ts - (CONV_W - 1):ts])

    y_p = xp.reshape(bp, tp, d)
    y_s = xs.reshape(bs, ROW_GROUP, d)[:, :ts]
    st = lambda key: jnp.stack(outs[key])
    return (y_p, y_s, st("k_p"), st("v_p"), st("k_s"), st("v_s"), st("hg_p"), st("hg_s"),
            st("gv_s"), st("cv_p"), st("cv_s"))
```
